```python
import jax, jax.numpy as jnp
from jax import lax
import numpy as np

D_MODEL = 2048
BATCH = 8
SEQ = 4096
DEPTH = 2

MIX_WIDTH = D_MODEL
MLA_HEADS = D_MODEL // 256
QK_NOPE_DIM = 128
QK_ROPE_DIM = 64
QK_HEAD_DIM = QK_NOPE_DIM + QK_ROPE_DIM
V_HEAD_DIM = 128
Q_LORA_RANK = 512
KV_LORA_RANK = 256
ATTN_WIDTH = MLA_HEADS * V_HEAD_DIM
GM_WIDTH = MIX_WIDTH - ATTN_WIDTH
GM_GROUPS = D_MODEL // 256
GM_GROUP_DIM = GM_WIDTH // GM_GROUPS
CHUNK = 128
D_FF = 128 * ((8 * D_MODEL // 3 + 127) // 128)
PLE_DIM = 256
ROPE_BASE = 10000.0
EPS = 1e-6
Q_BLOCK = 128
IN_COLS = Q_LORA_RANK + KV_LORA_RANK + QK_ROPE_DIM + 2 * GM_WIDTH
SPLITS = (Q_LORA_RANK,
          Q_LORA_RANK + KV_LORA_RANK,
          Q_LORA_RANK + KV_LORA_RANK + QK_ROPE_DIM,
          Q_LORA_RANK + KV_LORA_RANK + QK_ROPE_DIM + GM_WIDTH)

kernel_name = "hybrid_mla_gmlp_macaron_ple"


def rms_norm(x, g):
    xf = x.astype(jnp.float32)
    y = xf * lax.rsqrt(jnp.mean(xf * xf, axis=-1, keepdims=True) + EPS)
    return (y * g.astype(jnp.float32)).astype(x.dtype)


def swiglu(x, w1, w3, w2):
    return (jax.nn.silu(x @ w1) * (x @ w3)) @ w2


def rope_tables(positions):
    inv_freq = ROPE_BASE ** (-jnp.arange(0, QK_ROPE_DIM, 2, dtype=jnp.float32) / QK_ROPE_DIM)
    ang = positions.astype(jnp.float32)[..., None] * inv_freq
    return jnp.cos(ang)[:, :, None, :], jnp.sin(ang)[:, :, None, :]


def apply_rope(x, cos, sin):
    x1, x2 = jnp.split(x.astype(jnp.float32), 2, axis=-1)
    out = jnp.concatenate([x1 * cos - x2 * sin, x2 * cos + x1 * sin], axis=-1)
    return out.astype(x.dtype)


def causal_block_attention(q, k, v):
    b, s, h, dqk = q.shape
    nb = s // Q_BLOCK
    scale = dqk ** -0.5
    qb = q.reshape(b, nb, Q_BLOCK, h, dqk).transpose(1, 0, 2, 3, 4)
    key_pos = jnp.arange(s)
    neg = jnp.finfo(jnp.float32).min

    def one_block(args):
        q_blk, blk = args
        q_pos = blk * Q_BLOCK + jnp.arange(Q_BLOCK)
        scores = jnp.einsum('bqhd,bkhd->bhqk', q_blk, k,
                            preferred_element_type=jnp.float32) * scale
        mask = key_pos[None, :] <= q_pos[:, None]
        scores = jnp.where(mask[None, None], scores, neg)
        probs = jax.nn.softmax(scores, axis=-1).astype(v.dtype)
        return jnp.einsum('bhqk,bkhd->bqhd', probs, v)

    out = lax.map(one_block, (qb, jnp.arange(nb)))
    return out.transpose(1, 0, 2, 3, 4).reshape(b, s, h, v.shape[-1])


def mla_mixer(c_q, c_kv, k_rope_raw, cos, sin, q_a_norm, w_uq, kv_a_norm, w_ukv, q_norm, k_norm):
    b, s, _ = c_q.shape
    q = (rms_norm(c_q, q_a_norm) @ w_uq).reshape(b, s, MLA_HEADS, QK_HEAD_DIM)
    kv = (rms_norm(c_kv, kv_a_norm) @ w_ukv).reshape(b, s, MLA_HEADS, QK_NOPE_DIM + V_HEAD_DIM)
    k_nope, v = jnp.split(kv, [QK_NOPE_DIM], axis=-1)
    k_rope = jnp.broadcast_to(k_rope_raw[:, :, None, :], (b, s, MLA_HEADS, QK_ROPE_DIM))
    k = jnp.concatenate([k_nope, k_rope], axis=-1)
    q = rms_norm(q, q_norm)
    k = rms_norm(k, k_norm)
    q = jnp.concatenate([q[..., :QK_NOPE_DIM], apply_rope(q[..., QK_NOPE_DIM:], cos, sin)], axis=-1)
    k = jnp.concatenate([k[..., :QK_NOPE_DIM], apply_rope(k[..., QK_NOPE_DIM:], cos, sin)], axis=-1)
    return causal_block_attention(q, k, v).reshape(b, s, ATTN_WIDTH)


def gmlp_mixer(u, v, v_norm, w_s, b_s):
    b, s, _ = u.shape
    u = jax.nn.gelu(u)
    v = rms_norm(jax.nn.gelu(v), v_norm)
    vc = v.reshape(b, s // CHUNK, CHUNK, GM_GROUPS, GM_GROUP_DIM)
    tril = jnp.tril(jnp.ones((CHUNK, CHUNK), dtype=bool))
    w_causal = jnp.where(tril[None], w_s, jnp.zeros_like(w_s))
    gate = jnp.einsum('gts,bcsgd->bctgd', w_causal, vc) + b_s.T[None, None, :, :, None]
    return u * gate.reshape(b, s, GM_WIDTH)


def setup_inputs(seed: int = 0) -> dict:
    key = jax.random.key(seed)
    ks = jax.random.split(key, 32)
    f32 = jnp.float32

    def w(k, shape, fan_in):
        return jax.random.normal(k, shape, f32) * fan_in ** -0.5

    def g(k, shape):
        return 1.0 + 0.05 * jax.random.normal(k, shape, f32)

    L = DEPTH
    offsets = jax.random.randint(ks[2], (BATCH, 1), 0, 1024, dtype=jnp.int32)
    positions = (offsets + jnp.arange(SEQ, dtype=jnp.int32)[None, :]).astype(jnp.int32)
    return {
        "x": jax.random.normal(ks[0], (BATCH, SEQ, D_MODEL), f32),
        "p": jax.random.normal(ks[1], (DEPTH, BATCH, SEQ, PLE_DIM), f32),
        "positions": positions,
        "ffn_a_norm": g(ks[3], (L, D_MODEL)),
        "ffn_a_w1": w(ks[4], (L, D_MODEL, D_FF), D_MODEL),
        "ffn_a_w3": w(ks[5], (L, D_MODEL, D_FF), D_MODEL),
        "ffn_a_w2": w(ks[6], (L, D_FF, D_MODEL), D_FF),
        "mix_norm": g(ks[7], (L, D_MODEL)),
        "w_in": w(ks[8], (L, D_MODEL, IN_COLS), D_MODEL),
        "q_a_norm": g(ks[9], (L, Q_LORA_RANK)),
        "w_uq": w(ks[10], (L, Q_LORA_RANK, MLA_HEADS * QK_HEAD_DIM), Q_LORA_RANK),
        "kv_a_norm": g(ks[11], (L, KV_LORA_RANK)),
        "w_ukv": w(ks[12], (L, KV_LORA_RANK, MLA_HEADS * (QK_NOPE_DIM + V_HEAD_DIM)), KV_LORA_RANK),
        "q_norm": g(ks[13], (L, QK_HEAD_DIM)),
        "k_norm": g(ks[14], (L, QK_HEAD_DIM)),
        "gm_v_norm": g(ks[15], (L, GM_WIDTH)),
        "gm_ws": w(ks[16], (L, GM_GROUPS, CHUNK, CHUNK), CHUNK),
        "gm_bs": 1.0 + 0.1 * jax.random.normal(ks[17], (L, GM_GROUPS, CHUNK), f32),
        "attn_out_norm": g(ks[18], (L, ATTN_WIDTH)),
        "gm_out_norm": g(ks[19], (L, GM_WIDTH)),
        "w_out": w(ks[20], (L, MIX_WIDTH, D_MODEL), MIX_WIDTH),
        "ffn_b_norm": g(ks[21], (L, D_MODEL)),
        "ffn_b_w1": w(ks[22], (L, D_MODEL, D_FF), D_MODEL),
        "ffn_b_w3": w(ks[23], (L, D_MODEL, D_FF), D_MODEL),
        "ffn_b_w2": w(ks[24], (L, D_FF, D_MODEL), D_FF),
        "ple_gate_norm": g(ks[25], (L, D_MODEL)),
        "w_ple_gate": w(ks[26], (L, D_MODEL, D_MODEL), D_MODEL),
        "w_ple": w(ks[27], (L, PLE_DIM, D_MODEL), PLE_DIM),
        "ple_norm": g(ks[28], (L, D_MODEL)),
    }


def reference(x, p, positions, ffn_a_norm, ffn_a_w1, ffn_a_w3, ffn_a_w2, mix_norm, w_in,
              q_a_norm, w_uq, kv_a_norm, w_ukv, q_norm, k_norm, gm_v_norm, gm_ws, gm_bs,
              attn_out_norm, gm_out_norm, w_out, ffn_b_norm, ffn_b_w1, ffn_b_w3, ffn_b_w2,
              ple_gate_norm, w_ple_gate, w_ple, ple_norm):
    cos, sin = rope_tables(positions)
    h = x
    for i in range(DEPTH):
        h = h + 0.5 * swiglu(rms_norm(h, ffn_a_norm[i]), ffn_a_w1[i], ffn_a_w3[i], ffn_a_w2[i])
        z = rms_norm(h, mix_norm[i]) @ w_in[i]
        c_q, c_kv, k_rope_raw, u, v = jnp.split(z, SPLITS, axis=-1)
        a_out = mla_mixer(c_q, c_kv, k_rope_raw, cos, sin, q_a_norm[i], w_uq[i],
                          kv_a_norm[i], w_ukv[i], q_norm[i], k_norm[i])
        g_out = gmlp_mixer(u, v, gm_v_norm[i], gm_ws[i], gm_bs[i])
        mixed = jnp.concatenate([rms_norm(a_out, attn_out_norm[i]),
                                 rms_norm(g_out, gm_out_norm[i])], axis=-1)
        h = h + mixed @ w_out[i]
        h = h + 0.5 * swiglu(rms_norm(h, ffn_b_norm[i]), ffn_b_w1[i], ffn_b_w3[i], ffn_b_w2[i])
        e = rms_norm(p[i] @ w_ple[i], ple_norm[i])
        gate = jax.nn.sigmoid(rms_norm(h, ple_gate_norm[i]) @ w_ple_gate[i])
        h = h + gate * e
    return h
```

```python
import functools
import math

import jax
import jax.numpy as jnp
from jax import lax
from jax.experimental import pallas as pl
from jax.experimental.pallas import tpu as pltpu

F32 = jnp.float32
BF16 = jnp.bfloat16

MLA_HEADS = 8
QK_NOPE_DIM = 128
QK_ROPE_DIM = 64
QK_HEAD_DIM = QK_NOPE_DIM + QK_ROPE_DIM
V_HEAD_DIM = 128
Q_LORA_RANK = 512
KV_LORA_RANK = 256
GM_GROUPS = 8
GM_GROUP_DIM = 128
CHUNK = 128
ROPE_BASE = 10000.0
EPS = 1e-6

LANES = 128
V7X_VMEM_BYTES = 64 * 1024 * 1024
V7X_VMEM_LIMIT_CAP = 60000 * 1024

HEAD_PAD = 2 * LANES
NEG_BIG = -1e30
LOG2E = math.log2(math.e)


def _vmem_limit(nbytes):
    assert nbytes <= V7X_VMEM_LIMIT_CAP, nbytes
    return int(min(V7X_VMEM_LIMIT_CAP, max(nbytes, 16 * 1024 * 1024)))


def _rms(x, g):
    ms = jnp.mean(x * x, axis=-1, keepdims=True)
    return x * lax.rsqrt(ms + EPS) * g


def _rope_kernel(pos_ref, invf_ref, cmask_ref, smask_ref, cos_ref, sin_ref):
    ang = pos_ref[...] * invf_ref[...]
    cos_ref[...] = jnp.cos(ang) * cmask_ref[...]
    sin_ref[...] = jnp.sin(ang) * smask_ref[...]


def _rope_tables(pos_f, *, tm):
    t = pos_f.shape[0]
    half = QK_ROPE_DIM // 2
    inv_freq = ROPE_BASE ** (-jnp.arange(0, QK_ROPE_DIM, 2, dtype=F32) / QK_ROPE_DIM)
    invf = jnp.tile(inv_freq, LANES // half)[None, :]
    ones, zeros = jnp.ones((half,), F32), jnp.zeros((half,), F32)
    cmask = jnp.concatenate([ones, ones, zeros, zeros])[None, :]
    smask = jnp.concatenate([-ones, ones, zeros, zeros])[None, :]
    row = pl.BlockSpec((1, LANES), lambda i: (0, 0))
    tab = pl.BlockSpec((tm, LANES), lambda i: (i, 0))
    return pl.pallas_call(
        _rope_kernel,
        out_shape=(jax.ShapeDtypeStruct((t, LANES), F32), jax.ShapeDtypeStruct((t, LANES), F32)),
        grid=(t // tm,),
        in_specs=[pl.BlockSpec((tm, 1), lambda i: (i, 0)), row, row, row],
        out_specs=(tab, tab),
        compiler_params=pltpu.CompilerParams(dimension_semantics=("parallel",)),
        name="rope_tables",
    )(pos_f, invf, cmask, smask)


def _ffn_kernel(h_ref, g_ref, w13_ref, w2_ref, o_ref, n_ref, *, tf):
    k = pl.program_id(1)

    @pl.when(k == 0)
    def _():
        x = h_ref[...]
        n_ref[...] = _rms(x, g_ref[...]).astype(BF16)
        o_ref[...] = x

    ab = jnp.dot(n_ref[...], w13_ref[...], preferred_element_type=F32)
    a, b = ab[:, :tf], ab[:, tf:]
    mid = (0.5 * a) * jax.nn.sigmoid(a) * b
    o_ref[...] += jnp.dot(mid.astype(BF16), w2_ref[...], preferred_element_type=F32)


def _ffn(h, g, w13, w2, *, tm, tf):
    t, d = h.shape
    f = w2.shape[0]
    est = (2 * tm * d * 4) * 2 + tm * d * 2 + 2 * (d * 2 * tf * 2 + tf * d * 2) + 4 * tm * 2 * tf * 4
    return pl.pallas_call(
        functools.partial(_ffn_kernel, tf=tf),
        out_shape=jax.ShapeDtypeStruct((t, d), F32),
        grid=(t // tm, f // tf),
        in_specs=[
            pl.BlockSpec((tm, d), lambda i, k: (i, 0)),
            pl.BlockSpec((1, d), lambda i, k: (0, 0)),
            pl.BlockSpec((d, 2 * tf), lambda i, k: (0, k)),
            pl.BlockSpec((tf, d), lambda i, k: (k, 0)),
        ],
        out_specs=pl.BlockSpec((tm, d), lambda i, k: (i, 0)),
        scratch_shapes=[pltpu.VMEM((tm, d), BF16)],
        compiler_params=pltpu.CompilerParams(
            dimension_semantics=("parallel", "arbitrary"), vmem_limit_bytes=_vmem_limit(est)),
        name="ffn",
    )(h, g, w13, w2)


def _mix_in_kernel(h_ref, mixg_ref, win_ref, qag_ref, wuq_ref, kvag_ref, wukv_ref, qg_ref, kg_ref,
                   gvg_ref, ws_ref, bs_ref, gog_ref, cos_ref, sin_ref,
                   q_ref, k_ref, v_ref, gn_ref, *, tm):
    n = _rms(h_ref[...], mixg_ref[...]).astype(BF16)
    z = jnp.dot(n, win_ref[...], preferred_element_type=F32)
    o_kv = Q_LORA_RANK
    o_kr = o_kv + KV_LORA_RANK
    o_u = o_kr + LANES
    o_v = o_u + GM_GROUPS * GM_GROUP_DIM
    c_q, c_kv, kr = z[:, :o_kv], z[:, o_kv:o_kr], z[:, o_kr:o_u]
    u, v = z[:, o_u:o_v], z[:, o_v:]
    cos, sin = cos_ref[...], sin_ref[...]
    inv_hd = 1.0 / QK_HEAD_DIM

    def rope(x):
        return x * cos + pltpu.roll(x, QK_ROPE_DIM // 2, 1) * sin

    qf = jnp.dot(_rms(c_q, qag_ref[...]).astype(BF16), wuq_ref[...], preferred_element_type=F32)
    qg = qg_ref[...]
    q_scale = QK_HEAD_DIM ** -0.5 * LOG2E
    for hd in range(MLA_HEADS):
        nope = qf[:, hd * HEAD_PAD: hd * HEAD_PAD + LANES]
        rt = qf[:, hd * HEAD_PAD + LANES: (hd + 1) * HEAD_PAD]
        ss = jnp.sum(nope * nope + 0.5 * (rt * rt), axis=-1, keepdims=True)
        r = lax.rsqrt(ss * inv_hd + EPS)
        q_ref[:, hd * HEAD_PAD: hd * HEAD_PAD + LANES] = (nope * r * qg[:, :LANES] * q_scale).astype(BF16)
        q_ref[:, hd * HEAD_PAD + LANES: (hd + 1) * HEAD_PAD] = (
            rope(rt * r * qg[:, LANES:]) * q_scale).astype(BF16)

    kvf = jnp.dot(_rms(c_kv, kvag_ref[...]).astype(BF16), wukv_ref[...], preferred_element_type=F32)
    kg = kg_ref[...]
    kr_rot = rope(kr * kg[:, LANES:])
    ss_r = 0.5 * jnp.sum(kr * kr, axis=-1, keepdims=True)
    for hd in range(MLA_HEADS):
        kn = kvf[:, hd * HEAD_PAD: hd * HEAD_PAD + LANES]
        ss = jnp.sum(kn * kn, axis=-1, keepdims=True) + ss_r
        r = lax.rsqrt(ss * inv_hd + EPS)
        k_ref[:, hd * HEAD_PAD: hd * HEAD_PAD + LANES] = (kn * r * kg[:, :LANES]).astype(BF16)
        k_ref[:, hd * HEAD_PAD + LANES: (hd + 1) * HEAD_PAD] = (kr_rot * r).astype(BF16)
        v_ref[:, hd * V_HEAD_DIM: (hd + 1) * V_HEAD_DIM] = kvf[:, hd * HEAD_PAD + LANES: (hd + 1) * HEAD_PAD].astype(BF16)

    ug = jax.nn.gelu(u)
    vn = _rms(jax.nn.gelu(v), gvg_ref[...]).astype(BF16)
    nchunk = tm // CHUNK
    trow = lax.broadcasted_iota(jnp.int32, (CHUNK, CHUNK), 0)
    scol = lax.broadcasted_iota(jnp.int32, (CHUNK, CHUNK), 1)
    cols = []
    for g in range(GM_GROUPS):
        wc = jnp.where(scol <= trow, ws_ref[g], jnp.zeros((), BF16))
        lo, hi = g * GM_GROUP_DIM, (g + 1) * GM_GROUP_DIM
        rhs = jnp.concatenate([vn[c * CHUNK:(c + 1) * CHUNK, lo:hi] for c in range(nchunk)], axis=1)
        gate = jnp.dot(wc, rhs, preferred_element_type=F32) + bs_ref[:, g:g + 1]
        cols.append(jnp.concatenate(
            [gate[:, c * GM_GROUP_DIM:(c + 1) * GM_GROUP_DIM] for c in range(nchunk)], axis=0))
    g_out = ug * jnp.concatenate(cols, axis=1)
    gn_ref[...] = _rms(g_out, gog_ref[...]).astype(BF16)


def _mix_in(h, mixg, win, qag, wuq, kvag, wukv, qg, kg, gvg, ws, bs, gog, cos, sin, *, tm):
    t, d = h.shape
    zc = win.shape[1]
    qw, vw, gw = MLA_HEADS * HEAD_PAD, MLA_HEADS * V_HEAD_DIM, GM_GROUPS * GM_GROUP_DIM

    def const(shape):
        return pl.BlockSpec(shape, lambda i: (0,) * len(shape), pipeline_mode=pl.Buffered(1))

    def rows(width):
        return pl.BlockSpec((tm, width), lambda i: (i, 0))

    weights = (d * zc + Q_LORA_RANK * qw + KV_LORA_RANK * qw + GM_GROUPS * CHUNK * CHUNK) * 2
    est = weights + 2 * tm * d * 4 + 2 * tm * (2 * qw + vw + gw) * 2 + 4 * tm * LANES * 4 + 6 * tm * zc * 4
    return pl.pallas_call(
        functools.partial(_mix_in_kernel, tm=tm),
        out_shape=(jax.ShapeDtypeStruct((t, qw), BF16), jax.ShapeDtypeStruct((t, qw), BF16),
                   jax.ShapeDtypeStruct((t, vw), BF16), jax.ShapeDtypeStruct((t, gw), BF16)),
        grid=(t // tm,),
        in_specs=[rows(d), const((1, d)), const((d, zc)), const((1, Q_LORA_RANK)), const((Q_LORA_RANK, qw)),
                  const((1, KV_LORA_RANK)), const((KV_LORA_RANK, qw)), const((1, HEAD_PAD)), const((1, HEAD_PAD)),
                  const((1, gw)), const((GM_GROUPS, CHUNK, CHUNK)), const((CHUNK, GM_GROUPS)), const((1, gw)),
                  rows(LANES), rows(LANES)],
        out_specs=(rows(qw), rows(qw), rows(vw), rows(gw)),
        compiler_params=pltpu.CompilerParams(
            dimension_semantics=("parallel",), vmem_limit_bytes=_vmem_limit(est)),
        name="mix_in",
    )(h, mixg, win, qag, wuq, kvag, wukv, qg, kg, gvg, ws, bs, gog, cos, sin)


def _attn_kernel(q_ref, k_ref, v_ref, o_ref, m_ref, l_ref, acc_ref, *, tq):
    i = pl.program_id(2)
    q = q_ref[0]
    m_ref[...] = jnp.full(m_ref.shape, NEG_BIG, F32)
    l_ref[...] = jnp.zeros(l_ref.shape, F32)
    acc_ref[...] = jnp.zeros(acc_ref.shape, F32)

    def step(j, diagonal):
        start = pl.multiple_of(j * tq, tq)
        kj = k_ref[0, pl.ds(start, tq), :]
        vj = v_ref[0, pl.ds(start, tq), :]
        s = lax.dot_general(q, kj, (((1,), (1,)), ((), ())), preferred_element_type=F32)
        if diagonal:
            row = lax.broadcasted_iota(jnp.int32, (tq, tq), 0)
            col = lax.broadcasted_iota(jnp.int32, (tq, tq), 1)
            s = jnp.where(col <= row, s, NEG_BIG)
        m_prev = m_ref[...]
        m_new = jnp.maximum(m_prev, jnp.max(s, axis=-1, keepdims=True))
        p = jnp.exp2(s - m_new)
        alpha = jnp.exp2(m_prev - m_new)
        l_ref[...] = alpha * l_ref[...] + jnp.sum(p, axis=-1, keepdims=True)
        acc_ref[...] = alpha * acc_ref[...] + jnp.dot(p.astype(BF16), vj, preferred_element_type=F32)
        m_ref[...] = m_new

    def body(j, carry):
        step(j, False)
        return carry

    lax.fori_loop(0, i, body, 0)
    step(i, True)
    o_ref[0] = acc_ref[...] / l_ref[...]


def _attention(q, k, v, *, tq):
    b, s, _ = q.shape
    est = 2 * (tq * HEAD_PAD + s * HEAD_PAD + s * V_HEAD_DIM) * 2 + 2 * tq * V_HEAD_DIM * 4 \
        + 3 * tq * LANES * 4 + 4 * tq * tq * 4
    return pl.pallas_call(
        functools.partial(_attn_kernel, tq=tq),
        out_shape=jax.ShapeDtypeStruct((b, s, MLA_HEADS * V_HEAD_DIM), F32),
        grid=(b, MLA_HEADS, s // tq),
        in_specs=[
            pl.BlockSpec((1, tq, HEAD_PAD), lambda bi, hi, qi: (bi, qi, hi)),
            pl.BlockSpec((1, s, HEAD_PAD), lambda bi, hi, qi: (bi, 0, hi)),
            pl.BlockSpec((1, s, V_HEAD_DIM), lambda bi, hi, qi: (bi, 0, hi)),
        ],
        out_specs=pl.BlockSpec((1, tq, V_HEAD_DIM), lambda bi, hi, qi: (bi, qi, hi)),
        scratch_shapes=[pltpu.VMEM((tq, 1), F32), pltpu.VMEM((tq, 1), F32), pltpu.VMEM((tq, V_HEAD_DIM), F32)],
        compiler_params=pltpu.CompilerParams(
            dimension_semantics=("parallel", "parallel", "arbitrary"), vmem_limit_bytes=_vmem_limit(est)),
        name="attn",
    )(q, k, v)


def _mix_out_kernel(h_ref, a_ref, gn_ref, ag_ref, wout_ref, o_ref):
    an = _rms(a_ref[...], ag_ref[...]).astype(BF16)
    mixed = jnp.concatenate([an, gn_ref[...]], axis=1)
    o_ref[...] = h_ref[...] + jnp.dot(mixed, wout_ref[...], preferred_element_type=F32)


def _mix_out(h, a, gn, ag, wout, *, tm):
    t, d = h.shape
    aw, gw = a.shape[1], gn.shape[1]
    est = wout.size * 2 + 2 * tm * (2 * d * 4 + aw * 4 + gw * 2) + 3 * tm * d * 4
    return pl.pallas_call(
        _mix_out_kernel,
        out_shape=jax.ShapeDtypeStruct((t, d), F32),
        grid=(t // tm,),
        in_specs=[
            pl.BlockSpec((tm, d), lambda i: (i, 0)),
            pl.BlockSpec((tm, aw), lambda i: (i, 0)),
            pl.BlockSpec((tm, gw), lambda i: (i, 0)),
            pl.BlockSpec((1, aw), lambda i: (0, 0), pipeline_mode=pl.Buffered(1)),
            pl.BlockSpec((aw + gw, d), lambda i: (0, 0), pipeline_mode=pl.Buffered(1)),
        ],
        out_specs=pl.BlockSpec((tm, d), lambda i: (i, 0)),
        compiler_params=pltpu.CompilerParams(
            dimension_semantics=("parallel",), vmem_limit_bytes=_vmem_limit(est)),
        name="mix_out",
    )(h, a, gn, ag, wout)


def _ple_kernel(h_ref, p_ref, gg_ref, wg_ref, wp_ref, pg_ref, o_ref):
    x = h_ref[...]
    e = _rms(jnp.dot(p_ref[...].astype(BF16), wp_ref[...], preferred_element_type=F32), pg_ref[...])
    gate = jax.nn.sigmoid(
        jnp.dot(_rms(x, gg_ref[...]).astype(BF16), wg_ref[...], preferred_element_type=F32))
    o_ref[...] = x + gate * e


def _ple(h, p, layer, gg, wg, wp, pg, *, tm):
    t, d = h.shape
    pd = p.shape[-1]
    est = (wg.size + wp.size) * 2 + 2 * tm * (2 * d * 4 + pd * 4) + 4 * tm * d * 4
    return pl.pallas_call(
        _ple_kernel,
        out_shape=jax.ShapeDtypeStruct((t, d), F32),
        grid=(t // tm,),
        in_specs=[
            pl.BlockSpec((tm, d), lambda i: (i, 0)),
            pl.BlockSpec((None, tm, pd), lambda i: (layer, i, 0)),
            pl.BlockSpec((1, d), lambda i: (0, 0), pipeline_mode=pl.Buffered(1)),
            pl.BlockSpec((d, d), lambda i: (0, 0), pipeline_mode=pl.Buffered(1)),
            pl.BlockSpec((pd, d), lambda i: (0, 0), pipeline_mode=pl.Buffered(1)),
            pl.BlockSpec((1, d), lambda i: (0, 0), pipeline_mode=pl.Buffered(1)),
        ],
        out_specs=pl.BlockSpec((tm, d), lambda i: (i, 0)),
        compiler_params=pltpu.CompilerParams(
            dimension_semantics=("parallel",), vmem_limit_bytes=_vmem_limit(est)),
        name="ple",
    )(h, p, gg, wg, wp, pg)


def _ffn_weights(w1, w3, w2, *, tf):
    d, f = w1.shape
    fp = -(-f // tf) * tf
    pad = ((0, 0), (0, fp - f))
    w1p = jnp.pad(w1.astype(BF16), pad).reshape(d, fp // tf, tf)
    w3p = jnp.pad(w3.astype(BF16), pad).reshape(d, fp // tf, tf)
    w13 = jnp.concatenate([w1p, w3p], axis=2).reshape(d, 2 * fp)
    w2p = jnp.pad(w2.astype(BF16), ((0, fp - f), (0, 0)))
    return w13, w2p


def _dup_rope(x, axis_len_nope):
    return jnp.concatenate([x, x[..., axis_len_nope:]], axis=-1)


def kernel(x, p, positions, ffn_a_norm, ffn_a_w1, ffn_a_w3, ffn_a_w2, mix_norm, w_in, q_a_norm, w_uq, kv_a_norm, w_ukv, q_norm, k_norm, gm_v_norm, gm_ws, gm_bs, attn_out_norm, gm_out_norm, w_out, ffn_b_norm, ffn_b_w1, ffn_b_w3, ffn_b_w2, ple_gate_norm, w_ple_gate, w_ple, ple_norm):
    b, s, d = x.shape
    depth = p.shape[0]
    t = b * s
    tm_ffn, tf = 512, 512
    tm_mix, tm_out, tm_ple, tq = 256, 512, 512, 512

    cos, sin = _rope_tables(positions.astype(F32).reshape(t, 1), tm=1024)
    p2 = p.reshape(depth, t, p.shape[-1])
    h = x.reshape(t, d)
    o_kr = Q_LORA_RANK + KV_LORA_RANK
    for i in range(depth):
        w13, w2 = _ffn_weights(ffn_a_w1[i], ffn_a_w3[i], ffn_a_w2[i], tf=tf)
        h = _ffn(h, ffn_a_norm[i][None, :], w13, w2, tm=tm_ffn, tf=tf)

        win = w_in[i].astype(BF16)
        win = jnp.concatenate(
            [win[:, :o_kr + QK_ROPE_DIM], win[:, o_kr:o_kr + QK_ROPE_DIM], win[:, o_kr + QK_ROPE_DIM:]], axis=1)
        wuq = _dup_rope(w_uq[i].astype(BF16).reshape(Q_LORA_RANK, MLA_HEADS, QK_HEAD_DIM), QK_NOPE_DIM)
        wuq = wuq.reshape(Q_LORA_RANK, MLA_HEADS * HEAD_PAD)
        q, k, v, gn = _mix_in(
            h, mix_norm[i][None, :], win, q_a_norm[i][None, :], wuq, kv_a_norm[i][None, :],
            w_ukv[i].astype(BF16), _dup_rope(q_norm[i], QK_NOPE_DIM)[None, :],
            _dup_rope(k_norm[i], QK_NOPE_DIM)[None, :], gm_v_norm[i][None, :], gm_ws[i].astype(BF16),
            gm_bs[i].T, gm_out_norm[i][None, :], cos, sin, tm=tm_mix)
        a = _attention(q.reshape(b, s, -1), k.reshape(b, s, -1), v.reshape(b, s, -1), tq=tq)
        h = _mix_out(h, a.reshape(t, -1), gn, attn_out_norm[i][None, :], w_out[i].astype(BF16), tm=tm_out)

        w13, w2 = _ffn_weights(ffn_b_w1[i], ffn_b_w3[i], ffn_b_w2[i], tf=tf)
        h = _ffn(h, ffn_b_norm[i][None, :], w13, w2, tm=tm_ffn, tf=tf)

        h = _ple(h, p2, i, ple_gate_norm[i][None, :], w_ple_gate[i].astype(BF16), w_ple[i].astype(BF16),
                 ple_norm[i][None, :], tm=tm_ple)
    return h.reshape(b, s, d)
```

```python
import functools
import math

import jax
import jax.numpy as jnp
from jax import lax
from jax.experimental import pallas as pl
from jax.experimental.pallas import tpu as pltpu

F32 = jnp.float32
BF16 = jnp.bfloat16

MLA_HEADS = 8
QK_NOPE_DIM = 128
QK_ROPE_DIM = 64
QK_HEAD_DIM = QK_NOPE_DIM + QK_ROPE_DIM
V_HEAD_DIM = 128
Q_LORA_RANK = 512
KV_LORA_RANK = 256
GM_GROUPS = 8
GM_GROUP_DIM = 128
CHUNK = 128
ROPE_BASE = 10000.0
EPS = 1e-6

LANES = 128
V7X_VMEM_BYTES = 64 * 1024 * 1024
V7X_VMEM_LIMIT_CAP = 60000 * 1024

HEAD_PAD = 2 * LANES
NEG_BIG = -1e30
LOG2E = math.log2(math.e)


def _vmem_limit(nbytes):
    assert nbytes <= V7X_VMEM_LIMIT_CAP, nbytes
    return int(min(V7X_VMEM_LIMIT_CAP, max(nbytes, 16 * 1024 * 1024)))


def _rms(x, g):
    ms = jnp.mean(x * x, axis=-1, keepdims=True)
    return x * lax.rsqrt(ms + EPS) * g


def _rope_kernel(pos_ref, invf_ref, cmask_ref, smask_ref, cos_ref, sin_ref):
    ang = pos_ref[...] * invf_ref[...]
    cos_ref[...] = jnp.cos(ang) * cmask_ref[...]
    sin_ref[...] = jnp.sin(ang) * smask_ref[...]


def _rope_tables(pos_f, *, tm):
    t = pos_f.shape[0]
    half = QK_ROPE_DIM // 2
    inv_freq = ROPE_BASE ** (-jnp.arange(0, QK_ROPE_DIM, 2, dtype=F32) / QK_ROPE_DIM)
    invf = jnp.tile(inv_freq, LANES // half)[None, :]
    ones, zeros = jnp.ones((half,), F32), jnp.zeros((half,), F32)
    cmask = jnp.concatenate([ones, ones, zeros, zeros])[None, :]
    smask = jnp.concatenate([-ones, ones, zeros, zeros])[None, :]
    row = pl.BlockSpec((1, LANES), lambda i: (0, 0))
    tab = pl.BlockSpec((tm, LANES), lambda i: (i, 0))
    return pl.pallas_call(
        _rope_kernel,
        out_shape=(jax.ShapeDtypeStruct((t, LANES), F32), jax.ShapeDtypeStruct((t, LANES), F32)),
        grid=(t // tm,),
        in_specs=[pl.BlockSpec((tm, 1), lambda i: (i, 0)), row, row, row],
        out_specs=(tab, tab),
        compiler_params=pltpu.CompilerParams(dimension_semantics=("parallel",)),
        name="rope_tables",
    )(pos_f, invf, cmask, smask)


def _ffn_kernel(h_ref, g_ref, w1_ref, w3_ref, w2_ref, o_ref, n_ref):
    k = pl.program_id(1)

    @pl.when(k == 0)
    def _():
        x = h_ref[...]
        n_ref[...] = _rms(x, g_ref[...]).astype(BF16)
        o_ref[...] = x

    n = n_ref[...]
    a = jnp.dot(n, w1_ref[...], preferred_element_type=F32)
    b = jnp.dot(n, w3_ref[...], preferred_element_type=F32)
    mid = (0.5 * a) * jax.nn.sigmoid(a) * b
    o_ref[...] += jnp.dot(mid.astype(BF16), w2_ref[...], preferred_element_type=F32)


def _ffn(h, g, w1, w3, w2, *, tm, tf):
    t, d = h.shape
    f = w2.shape[0]
    est = tm * d * 4 + 2 * tm * d * 4 + tm * d * 2 + 2 * 3 * d * tf * 2 + 4 * tm * tf * 4
    return pl.pallas_call(
        _ffn_kernel,
        out_shape=jax.ShapeDtypeStruct((t, d), F32),
        grid=(t // tm, f // tf),
        in_specs=[
            pl.BlockSpec((tm, d), lambda i, k: (i, 0), pipeline_mode=pl.Buffered(1)),
            pl.BlockSpec((1, d), lambda i, k: (0, 0)),
            pl.BlockSpec((d, tf), lambda i, k: (0, k)),
            pl.BlockSpec((d, tf), lambda i, k: (0, k)),
            pl.BlockSpec((tf, d), lambda i, k: (k, 0)),
        ],
        out_specs=pl.BlockSpec((tm, d), lambda i, k: (i, 0)),
        scratch_shapes=[pltpu.VMEM((tm, d), BF16)],
        compiler_params=pltpu.CompilerParams(
            dimension_semantics=("parallel", "arbitrary"), vmem_limit_bytes=_vmem_limit(est)),
        name="ffn",
    )(h, g, w1, w3, w2)


def _mix_in_kernel(h_ref, mixg_ref, win_ref, qag_ref, wuq_ref, kvag_ref, wuk_ref, wvt_ref, qg_ref, kg_ref,
                   gvg_ref, ws_ref, bs_ref, gog_ref, cos_ref, sin_ref,
                   q_ref, k_ref, vt_ref, gn_ref, *, tm):
    n = _rms(h_ref[...], mixg_ref[...]).astype(BF16)
    z = jnp.dot(n, win_ref[...], preferred_element_type=F32)
    o_kv = Q_LORA_RANK
    o_kr = o_kv + KV_LORA_RANK
    o_u = o_kr + LANES
    o_v = o_u + GM_GROUPS * GM_GROUP_DIM
    c_q, c_kv, kr = z[:, :o_kv], z[:, o_kv:o_kr], z[:, o_kr:o_u]
    u, v = z[:, o_u:o_v], z[:, o_v:]
    cos, sin = cos_ref[...], sin_ref[...]
    inv_hd = 1.0 / QK_HEAD_DIM

    def rope(x):
        return x * cos + pltpu.roll(x, QK_ROPE_DIM // 2, 1) * sin

    qf = jnp.dot(_rms(c_q, qag_ref[...]).astype(BF16), wuq_ref[...], preferred_element_type=F32)
    qg = qg_ref[...]
    q_scale = QK_HEAD_DIM ** -0.5 * LOG2E
    for hd in range(MLA_HEADS):
        nope = qf[:, hd * HEAD_PAD: hd * HEAD_PAD + LANES]
        rt = qf[:, hd * HEAD_PAD + LANES: (hd + 1) * HEAD_PAD]
        ss = jnp.sum(nope * nope + 0.5 * (rt * rt), axis=-1, keepdims=True)
        r = lax.rsqrt(ss * inv_hd + EPS)
        q_ref[:, hd * HEAD_PAD: hd * HEAD_PAD + LANES] = (nope * r * qg[:, :LANES] * q_scale).astype(BF16)
        q_ref[:, hd * HEAD_PAD + LANES: (hd + 1) * HEAD_PAD] = (
            rope(rt * r * qg[:, LANES:]) * q_scale).astype(BF16)

    kvn = _rms(c_kv, kvag_ref[...]).astype(BF16)
    kf = jnp.dot(kvn, wuk_ref[...], preferred_element_type=F32)
    kg = kg_ref[...]
    kr_rot = rope(kr * kg[:, LANES:])
    ss_r = 0.5 * jnp.sum(kr * kr, axis=-1, keepdims=True)
    for hd in range(MLA_HEADS):
        kn = kf[:, hd * QK_NOPE_DIM: (hd + 1) * QK_NOPE_DIM]
        ss = jnp.sum(kn * kn, axis=-1, keepdims=True) + ss_r
        r = lax.rsqrt(ss * inv_hd + EPS)
        k_ref[:, hd * HEAD_PAD: hd * HEAD_PAD + LANES] = (kn * r * kg[:, :LANES]).astype(BF16)
        k_ref[:, hd * HEAD_PAD + LANES: (hd + 1) * HEAD_PAD] = (kr_rot * r).astype(BF16)
    vt_ref[0] = lax.dot_general(wvt_ref[...], kvn, (((1,), (1,)), ((), ())),
                                preferred_element_type=F32).astype(BF16)

    ug = jax.nn.gelu(u)
    vn = _rms(jax.nn.gelu(v), gvg_ref[...]).astype(BF16)
    nchunk = tm // CHUNK
    trow = lax.broadcasted_iota(jnp.int32, (CHUNK, CHUNK), 0)
    scol = lax.broadcasted_iota(jnp.int32, (CHUNK, CHUNK), 1)
    cols = []
    for g in range(GM_GROUPS):
        wc = jnp.where(scol <= trow, ws_ref[g], jnp.zeros((), BF16))
        lo, hi = g * GM_GROUP_DIM, (g + 1) * GM_GROUP_DIM
        rhs = jnp.concatenate([vn[c * CHUNK:(c + 1) * CHUNK, lo:hi] for c in range(nchunk)], axis=1)
        gate = jnp.dot(wc, rhs, preferred_element_type=F32) + bs_ref[:, g:g + 1]
        cols.append(jnp.concatenate(
            [gate[:, c * GM_GROUP_DIM:(c + 1) * GM_GROUP_DIM] for c in range(nchunk)], axis=0))
    g_out = ug * jnp.concatenate(cols, axis=1)
    gn_ref[...] = _rms(g_out, gog_ref[...]).astype(BF16)


def _mix_in(h, mixg, win, qag, wuq, kvag, wuk, wvt, qg, kg, gvg, ws, bs, gog, cos, sin, *, tm):
    t, d = h.shape
    zc = win.shape[1]
    qw, kw, vw, gw = MLA_HEADS * HEAD_PAD, MLA_HEADS * QK_NOPE_DIM, MLA_HEADS * V_HEAD_DIM, GM_GROUPS * GM_GROUP_DIM

    def const(shape):
        return pl.BlockSpec(shape, lambda i: (0,) * len(shape), pipeline_mode=pl.Buffered(1))

    def rows(width):
        return pl.BlockSpec((tm, width), lambda i: (i, 0))

    weights = (d * zc + Q_LORA_RANK * qw + KV_LORA_RANK * (kw + vw) + GM_GROUPS * CHUNK * CHUNK) * 2
    est = weights + 2 * tm * d * 4 + 2 * tm * (2 * qw + vw + gw) * 2 + 4 * tm * LANES * 4 + 3 * tm * zc * 4
    return pl.pallas_call(
        functools.partial(_mix_in_kernel, tm=tm),
        out_shape=(jax.ShapeDtypeStruct((t, qw), BF16), jax.ShapeDtypeStruct((t, qw), BF16),
                   jax.ShapeDtypeStruct((t // tm, vw, tm), BF16), jax.ShapeDtypeStruct((t, gw), BF16)),
        grid=(t // tm,),
        in_specs=[rows(d), const((1, d)), const((d, zc)), const((1, Q_LORA_RANK)), const((Q_LORA_RANK, qw)),
                  const((1, KV_LORA_RANK)), const((KV_LORA_RANK, kw)), const((vw, KV_LORA_RANK)),
                  const((1, HEAD_PAD)), const((1, HEAD_PAD)),
                  const((1, gw)), const((GM_GROUPS, CHUNK, CHUNK)), const((CHUNK, GM_GROUPS)), const((1, gw)),
                  rows(LANES), rows(LANES)],
        out_specs=(rows(qw), rows(qw), pl.BlockSpec((1, vw, tm), lambda i: (i, 0, 0)), rows(gw)),
        compiler_params=pltpu.CompilerParams(
            dimension_semantics=("parallel",), vmem_limit_bytes=_vmem_limit(est)),
        name="mix_in",
    )(h, mixg, win, qag, wuq, kvag, wuk, wvt, qg, kg, gvg, ws, bs, gog, cos, sin)


def _attn_kernel(q_ref, k_ref, vt_ref, o_ref, *acc_refs, tq, hg):
    i = pl.program_id(2)
    for acc_ref in acc_refs:
        acc_ref[...] = jnp.zeros(acc_ref.shape, F32)

    def step(j, carry, diagonal):
        ms, ls = carry
        start = pl.multiple_of(j * tq, tq)
        if diagonal:
            key = lax.broadcasted_iota(jnp.int32, (tq, tq), 0)
            qry = lax.broadcasted_iota(jnp.int32, (tq, tq), 1)
            keep = key <= qry
        sts = []
        for hd in range(hg):
            q = q_ref[0, :, hd * HEAD_PAD:(hd + 1) * HEAD_PAD]
            kj = k_ref[0, pl.ds(start, tq), hd * HEAD_PAD:(hd + 1) * HEAD_PAD]
            sts.append(lax.dot_general(kj, q, (((1,), (1,)), ((), ())), preferred_element_type=F32))
        new_ms, new_ls, pts, alphas = [], [], [], []
        for hd in range(hg):
            st = jnp.where(keep, sts[hd], NEG_BIG) if diagonal else sts[hd]
            m_new = jnp.maximum(ms[hd], jnp.max(st, axis=0, keepdims=True))
            pt = jnp.exp2(st - m_new)
            alpha = jnp.exp2(ms[hd] - m_new)
            new_ls.append(alpha * ls[hd] + jnp.sum(pt, axis=0, keepdims=True))
            new_ms.append(m_new)
            pts.append(pt.astype(BF16))
            alphas.append(alpha)
        for hd in range(hg):
            vt = vt_ref[j, hd * V_HEAD_DIM:(hd + 1) * V_HEAD_DIM, :]
            acc_refs[hd][...] = alphas[hd] * acc_refs[hd][...] + jnp.dot(
                vt, pts[hd], preferred_element_type=F32)
        return tuple(new_ms), tuple(new_ls)

    init = (tuple(jnp.full((1, tq), NEG_BIG, F32) for _ in range(hg)),
            tuple(jnp.zeros((1, tq), F32) for _ in range(hg)))
    carry = lax.fori_loop(0, i, lambda j, c: step(j, c, False), init)
    _, ls = step(i, carry, True)
    for hd in range(hg):
        o_ref[0, :, hd * V_HEAD_DIM:(hd + 1) * V_HEAD_DIM] = (acc_refs[hd][...] / ls[hd]).T


def _attention(q, k, vt, *, tq, hg):
    b, s, _ = q.shape
    nkv = s // tq
    est = 2 * hg * (tq * HEAD_PAD + s * HEAD_PAD + s * V_HEAD_DIM) * 2 + 2 * hg * tq * V_HEAD_DIM * 4 \
        + hg * tq * V_HEAD_DIM * 4 + 3 * hg * tq * tq * 4
    return pl.pallas_call(
        functools.partial(_attn_kernel, tq=tq, hg=hg),
        out_shape=jax.ShapeDtypeStruct((b, s, MLA_HEADS * V_HEAD_DIM), F32),
        grid=(b, MLA_HEADS // hg, nkv),
        in_specs=[
            pl.BlockSpec((1, tq, hg * HEAD_PAD), lambda bi, hi, qi: (bi, qi, hi)),
            pl.BlockSpec((1, s, hg * HEAD_PAD), lambda bi, hi, qi: (bi, 0, hi)),
            pl.BlockSpec((nkv, hg * V_HEAD_DIM, tq), lambda bi, hi, qi: (bi, hi, 0)),
        ],
        out_specs=pl.BlockSpec((1, tq, hg * V_HEAD_DIM), lambda bi, hi, qi: (bi, qi, hi)),
        scratch_shapes=[pltpu.VMEM((V_HEAD_DIM, tq), F32) for _ in range(hg)],
        compiler_params=pltpu.CompilerParams(
            dimension_semantics=("parallel", "parallel", "arbitrary"), vmem_limit_bytes=_vmem_limit(est)),
        name="attn",
    )(q, k, vt)


def _mix_out_kernel(h_ref, a_ref, gn_ref, ag_ref, wout_ref, o_ref):
    an = _rms(a_ref[...], ag_ref[...]).astype(BF16)
    mixed = jnp.concatenate([an, gn_ref[...]], axis=1)
    o_ref[...] = h_ref[...] + jnp.dot(mixed, wout_ref[...], preferred_element_type=F32)


def _mix_out(h, a, gn, ag, wout, *, tm):
    t, d = h.shape
    aw, gw = a.shape[1], gn.shape[1]
    est = wout.size * 2 + 2 * tm * (2 * d * 4 + aw * 4 + gw * 2) + 3 * tm * d * 4
    return pl.pallas_call(
        _mix_out_kernel,
        out_shape=jax.ShapeDtypeStruct((t, d), F32),
        grid=(t // tm,),
        in_specs=[
            pl.BlockSpec((tm, d), lambda i: (i, 0)),
            pl.BlockSpec((tm, aw), lambda i: (i, 0)),
            pl.BlockSpec((tm, gw), lambda i: (i, 0)),
            pl.BlockSpec((1, aw), lambda i: (0, 0), pipeline_mode=pl.Buffered(1)),
            pl.BlockSpec((aw + gw, d), lambda i: (0, 0), pipeline_mode=pl.Buffered(1)),
        ],
        out_specs=pl.BlockSpec((tm, d), lambda i: (i, 0)),
        compiler_params=pltpu.CompilerParams(
            dimension_semantics=("parallel",), vmem_limit_bytes=_vmem_limit(est)),
        name="mix_out",
    )(h, a, gn, ag, wout)


def _ple_kernel(h_ref, p_ref, gg_ref, wg_ref, wp_ref, pg_ref, o_ref):
    x = h_ref[...]
    e = _rms(jnp.dot(p_ref[...].astype(BF16), wp_ref[...], preferred_element_type=F32), pg_ref[...])
    gate = jax.nn.sigmoid(
        jnp.dot(_rms(x, gg_ref[...]).astype(BF16), wg_ref[...], preferred_element_type=F32))
    o_ref[...] = x + gate * e


def _ple(h, p, layer, gg, wg, wp, pg, *, tm):
    t, d = h.shape
    pd = p.shape[-1]
    est = (wg.size + wp.size) * 2 + 2 * tm * (2 * d * 4 + pd * 4) + 4 * tm * d * 4
    return pl.pallas_call(
        _ple_kernel,
        out_shape=jax.ShapeDtypeStruct((t, d), F32),
        grid=(t // tm,),
        in_specs=[
            pl.BlockSpec((tm, d), lambda i: (i, 0)),
            pl.BlockSpec((None, tm, pd), lambda i: (layer, i, 0)),
            pl.BlockSpec((1, d), lambda i: (0, 0), pipeline_mode=pl.Buffered(1)),
            pl.BlockSpec((d, d), lambda i: (0, 0), pipeline_mode=pl.Buffered(1)),
            pl.BlockSpec((pd, d), lambda i: (0, 0), pipeline_mode=pl.Buffered(1)),
            pl.BlockSpec((1, d), lambda i: (0, 0), pipeline_mode=pl.Buffered(1)),
        ],
        out_specs=pl.BlockSpec((tm, d), lambda i: (i, 0)),
        compiler_params=pltpu.CompilerParams(
            dimension_semantics=("parallel",), vmem_limit_bytes=_vmem_limit(est)),
        name="ple",
    )(h, p, gg, wg, wp, pg)


def _ffn_weights(w1, w3, w2, *, tf):
    f = w1.shape[1]
    fp = -(-f // tf) * tf
    cols = ((0, 0), (0, fp - f))
    return (jnp.pad(w1.astype(BF16), cols), jnp.pad(w3.astype(BF16), cols),
            jnp.pad(w2.astype(BF16), ((0, fp - f), (0, 0))))


def _dup_rope(x, axis_len_nope):
    return jnp.concatenate([x, x[..., axis_len_nope:]], axis=-1)


def kernel(x, p, positions, ffn_a_norm, ffn_a_w1, ffn_a_w3, ffn_a_w2, mix_norm, w_in, q_a_norm, w_uq, kv_a_norm, w_ukv, q_norm, k_norm, gm_v_norm, gm_ws, gm_bs, attn_out_norm, gm_out_norm, w_out, ffn_b_norm, ffn_b_w1, ffn_b_w3, ffn_b_w2, ple_gate_norm, w_ple_gate, w_ple, ple_norm):
    b, s, d = x.shape
    depth = p.shape[0]
    t = b * s
    tm_ffn, tf = 1024, 512
    tm_out, tm_ple, tq, heads_per_step = 512, 512, 512, 4

    cos, sin = _rope_tables(positions.astype(F32).reshape(t, 1), tm=1024)
    p2 = p.reshape(depth, t, p.shape[-1])
    h = x.reshape(t, d)
    o_kr = Q_LORA_RANK + KV_LORA_RANK
    for i in range(depth):
        w1, w3, w2 = _ffn_weights(ffn_a_w1[i], ffn_a_w3[i], ffn_a_w2[i], tf=tf)
        h = _ffn(h, ffn_a_norm[i][None, :], w1, w3, w2, tm=tm_ffn, tf=tf)

        win = w_in[i].astype(BF16)
        win = jnp.concatenate(
            [win[:, :o_kr + QK_ROPE_DIM], win[:, o_kr:o_kr + QK_ROPE_DIM], win[:, o_kr + QK_ROPE_DIM:]], axis=1)
        wuq = _dup_rope(w_uq[i].astype(BF16).reshape(Q_LORA_RANK, MLA_HEADS, QK_HEAD_DIM), QK_NOPE_DIM)
        wuq = wuq.reshape(Q_LORA_RANK, MLA_HEADS * HEAD_PAD)
        wukv = w_ukv[i].astype(BF16).reshape(KV_LORA_RANK, MLA_HEADS, QK_NOPE_DIM + V_HEAD_DIM)
        wuk = wukv[:, :, :QK_NOPE_DIM].reshape(KV_LORA_RANK, MLA_HEADS * QK_NOPE_DIM)
        wvt = wukv[:, :, QK_NOPE_DIM:].reshape(KV_LORA_RANK, MLA_HEADS * V_HEAD_DIM).T
        q, k, vt, gn = _mix_in(
            h, mix_norm[i][None, :], win, q_a_norm[i][None, :], wuq, kv_a_norm[i][None, :],
            wuk, wvt, _dup_rope(q_norm[i], QK_NOPE_DIM)[None, :],
            _dup_rope(k_norm[i], QK_NOPE_DIM)[None, :], gm_v_norm[i][None, :], gm_ws[i].astype(BF16),
            gm_bs[i].T, gm_out_norm[i][None, :], cos, sin, tm=tq)
        a = _attention(q.reshape(b, s, -1), k.reshape(b, s, -1), vt, tq=tq, hg=heads_per_step)
        h = _mix_out(h, a.reshape(t, -1), gn, attn_out_norm[i][None, :], w_out[i].astype(BF16), tm=tm_out)

        w1, w3, w2 = _ffn_weights(ffn_b_w1[i], ffn_b_w3[i], ffn_b_w2[i], tf=tf)
        h = _ffn(h, ffn_b_norm[i][None, :], w1, w3, w2, tm=tm_ffn, tf=tf)

        h = _ple(h, p2, i, ple_gate_norm[i][None, :], w_ple_gate[i].astype(BF16), w_ple[i].astype(BF16),
                 ple_norm[i][None, :], tm=tm_ple)
    return h.reshape(b, s, d)
```

```python
import functools
import math

import jax
import jax.numpy as jnp
from jax import lax
from jax.experimental import pallas as pl
from jax.experimental.pallas import tpu as pltpu

F32 = jnp.float32
BF16 = jnp.bfloat16

MLA_HEADS = 8
QK_NOPE_DIM = 128
QK_ROPE_DIM = 64
QK_HEAD_DIM = QK_NOPE_DIM + QK_ROPE_DIM
V_HEAD_DIM = 128
Q_LORA_RANK = 512
KV_LORA_RANK = 256
GM_GROUPS = 8
GM_GROUP_DIM = 128
CHUNK = 128
ROPE_BASE = 10000.0
EPS = 1e-6

LANES = 128
V7X_VMEM_BYTES = 64 * 1024 * 1024
V7X_VMEM_LIMIT_CAP = 60000 * 1024

HEAD_PAD = 2 * LANES
NEG_BIG = -1e30
LOG2E = math.log2(math.e)


def _vmem_limit(nbytes):
    assert nbytes <= V7X_VMEM_LIMIT_CAP, nbytes
    return int(min(V7X_VMEM_LIMIT_CAP, max(nbytes, 16 * 1024 * 1024)))


def _rms(x, g):
    ms = jnp.mean(x * x, axis=-1, keepdims=True)
    return x * lax.rsqrt(ms + EPS) * g


def _rope_kernel(pos_ref, invf_ref, cmask_ref, smask_ref, cos_ref, sin_ref):
    ang = pos_ref[...] * invf_ref[...]
    cos_ref[...] = jnp.cos(ang) * cmask_ref[...]
    sin_ref[...] = jnp.sin(ang) * smask_ref[...]


def _rope_tables(pos_f, *, tm):
    t = pos_f.shape[0]
    half = QK_ROPE_DIM // 2
    inv_freq = ROPE_BASE ** (-jnp.arange(0, QK_ROPE_DIM, 2, dtype=F32) / QK_ROPE_DIM)
    invf = jnp.tile(inv_freq, LANES // half)[None, :]
    ones, zeros = jnp.ones((half,), F32), jnp.zeros((half,), F32)
    cmask = jnp.concatenate([ones, ones, zeros, zeros])[None, :]
    smask = jnp.concatenate([-ones, ones, zeros, zeros])[None, :]
    row = pl.BlockSpec((1, LANES), lambda i: (0, 0))
    tab = pl.BlockSpec((tm, LANES), lambda i: (i, 0))
    return pl.pallas_call(
        _rope_kernel,
        out_shape=(jax.ShapeDtypeStruct((t, LANES), F32), jax.ShapeDtypeStruct((t, LANES), F32)),
        grid=(t // tm,),
        in_specs=[pl.BlockSpec((tm, 1), lambda i: (i, 0)), row, row, row],
        out_specs=(tab, tab),
        compiler_params=pltpu.CompilerParams(dimension_semantics=("parallel",)),
        name="rope_tables",
    )(pos_f, invf, cmask, smask)


def _ffn_kernel(h_ref, g_ref, w1_ref, w3_ref, w2_ref, o_ref, n_ref):
    k = pl.program_id(1)

    @pl.when(k == 0)
    def _():
        x = h_ref[...]
        n_ref[...] = _rms(x, g_ref[...]).astype(BF16)
        o_ref[...] = x

    n = n_ref[...]
    a = jnp.dot(n, w1_ref[...], preferred_element_type=F32)
    b = jnp.dot(n, w3_ref[...], preferred_element_type=F32)
    mid = (0.5 * a) * jax.nn.sigmoid(a) * b
    o_ref[...] += jnp.dot(mid.astype(BF16), w2_ref[...], preferred_element_type=F32)


def _ffn(h, g, w1, w3, w2, layer, *, tm, tf):
    t, d = h.shape
    f = w2.shape[1]
    est = 2 * tm * d * 4 + 2 * tm * d * 4 + tm * d * 2 + 2 * 3 * d * tf * 2 + 4 * tm * tf * 4
    return pl.pallas_call(
        _ffn_kernel,
        out_shape=jax.ShapeDtypeStruct((t, d), F32),
        grid=(t // tm, f // tf),
        in_specs=[
            pl.BlockSpec((tm, d), lambda i, k: (i, 0)),
            pl.BlockSpec((1, d), lambda i, k: (0, 0)),
            pl.BlockSpec((None, d, tf), lambda i, k: (layer, 0, k)),
            pl.BlockSpec((None, d, tf), lambda i, k: (layer, 0, k)),
            pl.BlockSpec((None, tf, d), lambda i, k: (layer, k, 0)),
        ],
        out_specs=pl.BlockSpec((tm, d), lambda i, k: (i, 0)),
        scratch_shapes=[pltpu.VMEM((tm, d), BF16)],
        compiler_params=pltpu.CompilerParams(
            dimension_semantics=("parallel", "arbitrary"), vmem_limit_bytes=_vmem_limit(est)),
        name="ffn",
    )(h, g, w1, w3, w2)


def _mix_in_kernel(h_ref, mixg_ref, win_ref, qag_ref, wuq_ref, kvag_ref, wuk_ref, wvt_ref, qg_ref, kg_ref,
                   gvg_ref, ws_ref, bs_ref, gog_ref, cos_ref, sin_ref,
                   q_ref, k_ref, vt_ref, gn_ref, *, tm):
    n = _rms(h_ref[...], mixg_ref[...]).astype(BF16)
    z = jnp.dot(n, win_ref[...], preferred_element_type=F32)
    o_kv = Q_LORA_RANK
    o_kr = o_kv + KV_LORA_RANK
    o_u = o_kr + LANES
    o_v = o_u + GM_GROUPS * GM_GROUP_DIM
    c_q, c_kv, kr = z[:, :o_kv], z[:, o_kv:o_kr], z[:, o_kr:o_u]
    u, v = z[:, o_u:o_v], z[:, o_v:]
    cos, sin = cos_ref[...], sin_ref[...]
    inv_hd = 1.0 / QK_HEAD_DIM

    def rope(x):
        return x * cos + pltpu.roll(x, QK_ROPE_DIM // 2, 1) * sin

    qf = jnp.dot(_rms(c_q, qag_ref[...]).astype(BF16), wuq_ref[...], preferred_element_type=F32)
    qg = qg_ref[...]
    q_scale = QK_HEAD_DIM ** -0.5 * LOG2E
    for hd in range(MLA_HEADS):
        nope = qf[:, hd * HEAD_PAD: hd * HEAD_PAD + LANES]
        rt = qf[:, hd * HEAD_PAD + LANES: (hd + 1) * HEAD_PAD]
        ss = jnp.sum(nope * nope + 0.5 * (rt * rt), axis=-1, keepdims=True)
        r = lax.rsqrt(ss * inv_hd + EPS)
        q_ref[:, hd * HEAD_PAD: hd * HEAD_PAD + LANES] = (nope * r * qg[:, :LANES] * q_scale).astype(BF16)
        q_ref[:, hd * HEAD_PAD + LANES: (hd + 1) * HEAD_PAD] = (
            rope(rt * r * qg[:, LANES:]) * q_scale).astype(BF16)

    kvn = _rms(c_kv, kvag_ref[...]).astype(BF16)
    kf = jnp.dot(kvn, wuk_ref[...], preferred_element_type=F32)
    kg = kg_ref[...]
    kr_rot = rope(kr * kg[:, LANES:])
    ss_r = 0.5 * jnp.sum(kr * kr, axis=-1, keepdims=True)
    for hd in range(MLA_HEADS):
        kn = kf[:, hd * QK_NOPE_DIM: (hd + 1) * QK_NOPE_DIM]
        ss = jnp.sum(kn * kn, axis=-1, keepdims=True) + ss_r
        r = lax.rsqrt(ss * inv_hd + EPS)
        k_ref[:, hd * HEAD_PAD: hd * HEAD_PAD + LANES] = (kn * r * kg[:, :LANES]).astype(BF16)
        k_ref[:, hd * HEAD_PAD + LANES: (hd + 1) * HEAD_PAD] = (kr_rot * r).astype(BF16)
    vt_ref[0] = lax.dot_general(wvt_ref[...], kvn, (((1,), (1,)), ((), ())),
                                preferred_element_type=F32).astype(BF16)

    ug = jax.nn.gelu(u)
    vn = _rms(jax.nn.gelu(v), gvg_ref[...]).astype(BF16)
    nchunk = tm // CHUNK
    trow = lax.broadcasted_iota(jnp.int32, (CHUNK, CHUNK), 0)
    scol = lax.broadcasted_iota(jnp.int32, (CHUNK, CHUNK), 1)
    cols = []
    for g in range(GM_GROUPS):
        wc = jnp.where(scol <= trow, ws_ref[g], jnp.zeros((), BF16))
        lo, hi = g * GM_GROUP_DIM, (g + 1) * GM_GROUP_DIM
        rhs = jnp.concatenate([vn[c * CHUNK:(c + 1) * CHUNK, lo:hi] for c in range(nchunk)], axis=1)
        gate = jnp.dot(wc, rhs, preferred_element_type=F32) + bs_ref[:, g:g + 1]
        cols.append(jnp.concatenate(
            [gate[:, c * GM_GROUP_DIM:(c + 1) * GM_GROUP_DIM] for c in range(nchunk)], axis=0))
    g_out = ug * jnp.concatenate(cols, axis=1)
    gn_ref[...] = _rms(g_out, gog_ref[...]).astype(BF16)


def _mix_in(h, mixg, win, qag, wuq, kvag, wuk, wvt, qg, kg, gvg, ws, bs, gog, cos, sin, *, tm):
    t, d = h.shape
    zc = win.shape[1]
    qw, kw, vw, gw = MLA_HEADS * HEAD_PAD, MLA_HEADS * QK_NOPE_DIM, MLA_HEADS * V_HEAD_DIM, GM_GROUPS * GM_GROUP_DIM

    def const(shape):
        return pl.BlockSpec(shape, lambda i: (0,) * len(shape), pipeline_mode=pl.Buffered(1))

    def rows(width):
        return pl.BlockSpec((tm, width), lambda i: (i, 0))

    weights = (d * zc + Q_LORA_RANK * qw + KV_LORA_RANK * (kw + vw) + GM_GROUPS * CHUNK * CHUNK) * 2
    est = weights + 2 * tm * d * 4 + 2 * tm * (2 * qw + vw + gw) * 2 + 4 * tm * LANES * 4 + 3 * tm * zc * 4
    return pl.pallas_call(
        functools.partial(_mix_in_kernel, tm=tm),
        out_shape=(jax.ShapeDtypeStruct((t, qw), BF16), jax.ShapeDtypeStruct((t, qw), BF16),
                   jax.ShapeDtypeStruct((t // tm, vw, tm), BF16), jax.ShapeDtypeStruct((t, gw), BF16)),
        grid=(t // tm,),
        in_specs=[rows(d), const((1, d)), const((d, zc)), const((1, Q_LORA_RANK)), const((Q_LORA_RANK, qw)),
                  const((1, KV_LORA_RANK)), const((KV_LORA_RANK, kw)), const((vw, KV_LORA_RANK)),
                  const((1, HEAD_PAD)), const((1, HEAD_PAD)),
                  const((1, gw)), const((GM_GROUPS, CHUNK, CHUNK)), const((CHUNK, GM_GROUPS)), const((1, gw)),
                  rows(LANES), rows(LANES)],
        out_specs=(rows(qw), rows(qw), pl.BlockSpec((1, vw, tm), lambda i: (i, 0, 0)), rows(gw)),
        compiler_params=pltpu.CompilerParams(
            dimension_semantics=("parallel",), vmem_limit_bytes=_vmem_limit(est)),
        name="mix_in",
    )(h, mixg, win, qag, wuq, kvag, wuk, wvt, qg, kg, gvg, ws, bs, gog, cos, sin)


def _attn_kernel(q_ref, k_ref, vt_ref, o_ref, *acc_refs, tq, hg):
    i = pl.program_id(2)
    for acc_ref in acc_refs:
        acc_ref[...] = jnp.zeros(acc_ref.shape, F32)

    def step(j, carry, diagonal):
        ms, ls = carry
        start = pl.multiple_of(j * tq, tq)
        if diagonal:
            key = lax.broadcasted_iota(jnp.int32, (tq, tq), 0)
            qry = lax.broadcasted_iota(jnp.int32, (tq, tq), 1)
            keep = key <= qry
        sts = []
        for hd in range(hg):
            q = q_ref[0, :, hd * HEAD_PAD:(hd + 1) * HEAD_PAD]
            kj = k_ref[0, pl.ds(start, tq), hd * HEAD_PAD:(hd + 1) * HEAD_PAD]
            sts.append(lax.dot_general(kj, q, (((1,), (1,)), ((), ())), preferred_element_type=F32))
        new_ms, new_ls, pts, alphas = [], [], [], []
        for hd in range(hg):
            st = jnp.where(keep, sts[hd], NEG_BIG) if diagonal else sts[hd]
            m_new = jnp.maximum(ms[hd], jnp.max(st, axis=0, keepdims=True))
            pt = jnp.exp2(st - m_new)
            alpha = jnp.exp2(ms[hd] - m_new)
            new_ls.append(alpha * ls[hd] + jnp.sum(pt, axis=0, keepdims=True))
            new_ms.append(m_new)
            pts.append(pt.astype(BF16))
            alphas.append(alpha)
        for hd in range(hg):
            vt = vt_ref[j, hd * V_HEAD_DIM:(hd + 1) * V_HEAD_DIM, :]
            acc_refs[hd][...] = alphas[hd] * acc_refs[hd][...] + jnp.dot(
                vt, pts[hd], preferred_element_type=F32)
        return tuple(new_ms), tuple(new_ls)

    init = (tuple(jnp.full((1, tq), NEG_BIG, F32) for _ in range(hg)),
            tuple(jnp.zeros((1, tq), F32) for _ in range(hg)))
    carry = lax.fori_loop(0, i, lambda j, c: step(j, c, False), init)
    _, ls = step(i, carry, True)
    for hd in range(hg):
        o_ref[0, :, hd * V_HEAD_DIM:(hd + 1) * V_HEAD_DIM] = (acc_refs[hd][...] / ls[hd]).T


def _attention(q, k, vt, *, tq, hg):
    b, s, _ = q.shape
    nkv = s // tq
    est = 2 * hg * (tq * HEAD_PAD + s * HEAD_PAD + s * V_HEAD_DIM) * 2 + 2 * hg * tq * V_HEAD_DIM * 4 \
        + hg * tq * V_HEAD_DIM * 4 + 3 * hg * tq * tq * 4
    return pl.pallas_call(
        functools.partial(_attn_kernel, tq=tq, hg=hg),
        out_shape=jax.ShapeDtypeStruct((b, s, MLA_HEADS * V_HEAD_DIM), F32),
        grid=(b, MLA_HEADS // hg, nkv),
        in_specs=[
            pl.BlockSpec((1, tq, hg * HEAD_PAD), lambda bi, hi, qi: (bi, qi, hi)),
            pl.BlockSpec((1, s, hg * HEAD_PAD), lambda bi, hi, qi: (bi, 0, hi)),
            pl.BlockSpec((nkv, hg * V_HEAD_DIM, tq), lambda bi, hi, qi: (bi, hi, 0)),
        ],
        out_specs=pl.BlockSpec((1, tq, hg * V_HEAD_DIM), lambda bi, hi, qi: (bi, qi, hi)),
        scratch_shapes=[pltpu.VMEM((V_HEAD_DIM, tq), F32) for _ in range(hg)],
        compiler_params=pltpu.CompilerParams(
            dimension_semantics=("parallel", "parallel", "arbitrary"), vmem_limit_bytes=_vmem_limit(est)),
        name="attn",
    )(q, k, vt)


def _mix_out_kernel(h_ref, a_ref, gn_ref, ag_ref, wout_ref, o_ref):
    an = _rms(a_ref[...], ag_ref[...]).astype(BF16)
    mixed = jnp.concatenate([an, gn_ref[...]], axis=1)
    o_ref[...] = h_ref[...] + jnp.dot(mixed, wout_ref[...], preferred_element_type=F32)


def _mix_out(h, a, gn, ag, wout, *, tm):
    t, d = h.shape
    aw, gw = a.shape[1], gn.shape[1]
    est = wout.size * 2 + 2 * tm * (2 * d * 4 + aw * 4 + gw * 2) + 3 * tm * d * 4
    return pl.pallas_call(
        _mix_out_kernel,
        out_shape=jax.ShapeDtypeStruct((t, d), F32),
        grid=(t // tm,),
        in_specs=[
            pl.BlockSpec((tm, d), lambda i: (i, 0)),
            pl.BlockSpec((tm, aw), lambda i: (i, 0)),
            pl.BlockSpec((tm, gw), lambda i: (i, 0)),
            pl.BlockSpec((1, aw), lambda i: (0, 0), pipeline_mode=pl.Buffered(1)),
            pl.BlockSpec((aw + gw, d), lambda i: (0, 0), pipeline_mode=pl.Buffered(1)),
        ],
        out_specs=pl.BlockSpec((tm, d), lambda i: (i, 0)),
        compiler_params=pltpu.CompilerParams(
            dimension_semantics=("parallel",), vmem_limit_bytes=_vmem_limit(est)),
        name="mix_out",
    )(h, a, gn, ag, wout)


def _ple_kernel(h_ref, p_ref, gg_ref, wg_ref, wp_ref, pg_ref, o_ref):
    x = h_ref[...]
    e = _rms(jnp.dot(p_ref[...].astype(BF16), wp_ref[...], preferred_element_type=F32), pg_ref[...])
    gate = jax.nn.sigmoid(
        jnp.dot(_rms(x, gg_ref[...]).astype(BF16), wg_ref[...], preferred_element_type=F32))
    o_ref[...] = x + gate * e


def _ple(h, p, layer, gg, wg, wp, pg, *, tm):
    t, d = h.shape
    pd = p.shape[-1]
    est = (wg.size + wp.size) * 2 + 2 * tm * (2 * d * 4 + pd * 4) + 4 * tm * d * 4
    return pl.pallas_call(
        _ple_kernel,
        out_shape=jax.ShapeDtypeStruct((t, d), F32),
        grid=(t // tm,),
        in_specs=[
            pl.BlockSpec((tm, d), lambda i: (i, 0)),
            pl.BlockSpec((None, tm, pd), lambda i: (layer, i, 0)),
            pl.BlockSpec((1, d), lambda i: (0, 0), pipeline_mode=pl.Buffered(1)),
            pl.BlockSpec((d, d), lambda i: (0, 0), pipeline_mode=pl.Buffered(1)),
            pl.BlockSpec((pd, d), lambda i: (0, 0), pipeline_mode=pl.Buffered(1)),
            pl.BlockSpec((1, d), lambda i: (0, 0), pipeline_mode=pl.Buffered(1)),
        ],
        out_specs=pl.BlockSpec((tm, d), lambda i: (i, 0)),
        compiler_params=pltpu.CompilerParams(
            dimension_semantics=("parallel",), vmem_limit_bytes=_vmem_limit(est)),
        name="ple",
    )(h, p, gg, wg, wp, pg)


def _cast_pad_kernel(w_ref, o_ref, *, axis, size):
    x = w_ref[...].astype(BF16)
    if axis == 0:
        o_ref[:size, :] = x
        o_ref[size:, :] = jnp.zeros((o_ref.shape[0] - size, o_ref.shape[1]), BF16)
    else:
        o_ref[:, :size] = x
        o_ref[:, size:] = jnp.zeros((o_ref.shape[0], o_ref.shape[1] - size), BF16)


def _cast_pad(w, *, axis, padded, blk):
    nl, r, c = w.shape
    if axis == 0:
        in_blk, out_blk, out_shape, grid = (None, r, blk), (None, padded, blk), (nl, padded, c), (nl, c // blk)
        idx = lambda l, j: (l, 0, j)
    else:
        in_blk, out_blk, out_shape, grid = (None, blk, c), (None, blk, padded), (nl, r, padded), (nl, r // blk)
        idx = lambda l, j: (l, j, 0)
    in_bytes, out_bytes = (r, c)[axis] * blk * 4, padded * blk * 2
    est = 3 * in_bytes + 2 * out_bytes
    return pl.pallas_call(
        functools.partial(_cast_pad_kernel, axis=axis, size=(r, c)[axis]),
        out_shape=jax.ShapeDtypeStruct(out_shape, BF16),
        grid=grid,
        in_specs=[pl.BlockSpec(in_blk, idx)],
        out_specs=pl.BlockSpec(out_blk, idx),
        compiler_params=pltpu.CompilerParams(
            dimension_semantics=("parallel", "parallel"), vmem_limit_bytes=_vmem_limit(est)),
        name="cast_pad",
    )(w)


def _ffn_weights(w1, w3, w2, *, tf):
    f = w1.shape[2]
    fp = -(-f // tf) * tf
    return (_cast_pad(w1, axis=1, padded=fp, blk=256), _cast_pad(w3, axis=1, padded=fp, blk=256),
            _cast_pad(w2, axis=0, padded=fp, blk=256))


def _dup_rope(x, axis_len_nope):
    return jnp.concatenate([x, x[..., axis_len_nope:]], axis=-1)


def kernel(x, p, positions, ffn_a_norm, ffn_a_w1, ffn_a_w3, ffn_a_w2, mix_norm, w_in, q_a_norm, w_uq, kv_a_norm, w_ukv, q_norm, k_norm, gm_v_norm, gm_ws, gm_bs, attn_out_norm, gm_out_norm, w_out, ffn_b_norm, ffn_b_w1, ffn_b_w3, ffn_b_w2, ple_gate_norm, w_ple_gate, w_ple, ple_norm):
    b, s, d = x.shape
    depth = p.shape[0]
    t = b * s
    tf = 512
    ffn_a = _ffn_weights(ffn_a_w1, ffn_a_w3, ffn_a_w2, tf=tf)
    ffn_b = _ffn_weights(ffn_b_w1, ffn_b_w3, ffn_b_w2, tf=tf)
    tm_out, tm_ple, tq, heads_per_step = 512, 512, 512, 4

    cos, sin = _rope_tables(positions.astype(F32).reshape(t, 1), tm=1024)
    p2 = p.reshape(depth, t, p.shape[-1])
    h = x.reshape(t, d)
    o_kr = Q_LORA_RANK + KV_LORA_RANK
    for i in range(depth):
        h = _ffn(h, ffn_a_norm[i][None, :], *ffn_a, i, tm=512, tf=tf)

        win = w_in[i].astype(BF16)
        win = jnp.concatenate(
            [win[:, :o_kr + QK_ROPE_DIM], win[:, o_kr:o_kr + QK_ROPE_DIM], win[:, o_kr + QK_ROPE_DIM:]], axis=1)
        wuq = _dup_rope(w_uq[i].astype(BF16).reshape(Q_LORA_RANK, MLA_HEADS, QK_HEAD_DIM), QK_NOPE_DIM)
        wuq = wuq.reshape(Q_LORA_RANK, MLA_HEADS * HEAD_PAD)
        wukv = w_ukv[i].astype(BF16).reshape(KV_LORA_RANK, MLA_HEADS, QK_NOPE_DIM + V_HEAD_DIM)
        wuk = wukv[:, :, :QK_NOPE_DIM].reshape(KV_LORA_RANK, MLA_HEADS * QK_NOPE_DIM)
        wvt = wukv[:, :, QK_NOPE_DIM:].reshape(KV_LORA_RANK, MLA_HEADS * V_HEAD_DIM).T
        q, k, vt, gn = _mix_in(
            h, mix_norm[i][None, :], win, q_a_norm[i][None, :], wuq, kv_a_norm[i][None, :],
            wuk, wvt, _dup_rope(q_norm[i], QK_NOPE_DIM)[None, :],
            _dup_rope(k_norm[i], QK_NOPE_DIM)[None, :], gm_v_norm[i][None, :], gm_ws[i].astype(BF16),
            gm_bs[i].T, gm_out_norm[i][None, :], cos, sin, tm=tq)
        a = _attention(q.reshape(b, s, -1), k.reshape(b, s, -1), vt, tq=tq, hg=heads_per_step)
        h = _mix_out(h, a.reshape(t, -1), gn, attn_out_norm[i][None, :], w_out[i].astype(BF16), tm=tm_out)

        h = _ffn(h, ffn_b_norm[i][None, :], *ffn_b, i, tm=1024, tf=tf)

        h = _ple(h, p2, i, ple_gate_norm[i][None, :], w_ple_gate[i].astype(BF16), w_ple[i].astype(BF16),
                 ple_norm[i][None, :], tm=tm_ple)
    return h.reshape(b, s, d)
```

```python
import functools
import math

import jax
import jax.numpy as jnp
from jax import lax
from jax.experimental import pallas as pl
from jax.experimental.pallas import tpu as pltpu

F32 = jnp.float32
BF16 = jnp.bfloat16

MLA_HEADS = 8
QK_NOPE_DIM = 128
QK_ROPE_DIM = 64
QK_HEAD_DIM = QK_NOPE_DIM + QK_ROPE_DIM
V_HEAD_DIM = 128
Q_LORA_RANK = 512
KV_LORA_RANK = 256
GM_GROUPS = 8
GM_GROUP_DIM = 128
CHUNK = 128
ROPE_BASE = 10000.0
EPS = 1e-6

LANES = 128
V7X_VMEM_BYTES = 64 * 1024 * 1024
V7X_VMEM_LIMIT_CAP = 60000 * 1024

HEAD_PAD = 2 * LANES
NEG_BIG = -1e30
SUM_ROWS = 16
LOG2E = math.log2(math.e)


def _vmem_limit(nbytes):
    assert nbytes <= V7X_VMEM_LIMIT_CAP, nbytes
    return int(min(V7X_VMEM_LIMIT_CAP, max(nbytes, 16 * 1024 * 1024)))


def _rms(x, g):
    ms = jnp.mean(x * x, axis=-1, keepdims=True)
    return x * lax.rsqrt(ms + EPS) * g


def _rope_kernel(pos_ref, invf_ref, cmask_ref, smask_ref, cos_ref, sin_ref):
    ang = pos_ref[...] * invf_ref[...]
    cos_ref[...] = jnp.cos(ang) * cmask_ref[...]
    sin_ref[...] = jnp.sin(ang) * smask_ref[...]


def _rope_tables(pos_f, *, tm):
    t = pos_f.shape[0]
    half = QK_ROPE_DIM // 2
    inv_freq = ROPE_BASE ** (-jnp.arange(0, QK_ROPE_DIM, 2, dtype=F32) / QK_ROPE_DIM)
    invf = jnp.tile(inv_freq, LANES // half)[None, :]
    ones, zeros = jnp.ones((half,), F32), jnp.zeros((half,), F32)
    cmask = jnp.concatenate([ones, ones, zeros, zeros])[None, :]
    smask = jnp.concatenate([-ones, ones, zeros, zeros])[None, :]
    row = pl.BlockSpec((1, LANES), lambda i: (0, 0))
    tab = pl.BlockSpec((tm, LANES), lambda i: (i, 0))
    return pl.pallas_call(
        _rope_kernel,
        out_shape=(jax.ShapeDtypeStruct((t, LANES), F32), jax.ShapeDtypeStruct((t, LANES), F32)),
        grid=(t // tm,),
        in_specs=[pl.BlockSpec((tm, 1), lambda i: (i, 0)), row, row, row],
        out_specs=(tab, tab),
        compiler_params=pltpu.CompilerParams(dimension_semantics=("parallel",)),
        name="rope_tables",
    )(pos_f, invf, cmask, smask)


def _ffn_kernel(h_ref, g_ref, w1_ref, w3_ref, w2_ref, o_ref, n_ref):
    k = pl.program_id(1)

    def half_swiglu(n):
        a = jnp.dot(n, w1_ref[...], preferred_element_type=F32)
        b = jnp.dot(n, w3_ref[...], preferred_element_type=F32)
        mid = (0.5 * a) * jax.nn.sigmoid(a) * b
        return jnp.dot(mid.astype(BF16), w2_ref[...], preferred_element_type=F32)

    @pl.when(k == 0)
    def _():
        x = h_ref[...]
        n = _rms(x, g_ref[...]).astype(BF16)
        n_ref[...] = n
        o_ref[...] = x + half_swiglu(n)

    @pl.when(k > 0)
    def _():
        o_ref[...] += half_swiglu(n_ref[...])


def _ffn(h, g, w1, w3, w2, layer, *, tm, tf):
    t, d = h.shape
    f = w2.shape[1]
    est = 2 * tm * d * 4 + 2 * tm * d * 4 + tm * d * 2 + 2 * 3 * d * tf * 2 + 4 * tm * tf * 4
    return pl.pallas_call(
        _ffn_kernel,
        out_shape=jax.ShapeDtypeStruct((t, d), F32),
        grid=(t // tm, f // tf),
        in_specs=[
            pl.BlockSpec((tm, d), lambda i, k: (i, 0)),
            pl.BlockSpec((1, d), lambda i, k: (0, 0)),
            pl.BlockSpec((None, d, tf), lambda i, k: (layer, 0, k)),
            pl.BlockSpec((None, d, tf), lambda i, k: (layer, 0, k)),
            pl.BlockSpec((None, tf, d), lambda i, k: (layer, k, 0)),
        ],
        out_specs=pl.BlockSpec((tm, d), lambda i, k: (i, 0)),
        scratch_shapes=[pltpu.VMEM((tm, d), BF16)],
        compiler_params=pltpu.CompilerParams(
            dimension_semantics=("parallel", "arbitrary"), vmem_limit_bytes=_vmem_limit(est)),
        name="ffn",
    )(h, g, w1, w3, w2)


def _mix_in_kernel(h_ref, mixg_ref, win_ref, qag_ref, wuq_ref, kvag_ref, wuk_ref, wvt_ref, qg_ref, kg_ref,
                   gvg_ref, ws_ref, bs_ref, gog_ref, cos_ref, sin_ref,
                   q_ref, k_ref, vt_ref, gn_ref, *, tm):
    n = _rms(h_ref[...], mixg_ref[...]).astype(BF16)
    z = jnp.dot(n, win_ref[...], preferred_element_type=F32)
    o_kv = Q_LORA_RANK
    o_kr = o_kv + KV_LORA_RANK
    o_u = o_kr + LANES
    o_v = o_u + GM_GROUPS * GM_GROUP_DIM
    c_q, c_kv, kr = z[:, :o_kv], z[:, o_kv:o_kr], z[:, o_kr:o_u]
    u, v = z[:, o_u:o_v], z[:, o_v:]
    cos, sin = cos_ref[...], sin_ref[...]
    inv_hd = 1.0 / QK_HEAD_DIM

    def rope(x):
        return x * cos + pltpu.roll(x, QK_ROPE_DIM // 2, 1) * sin

    qf = jnp.dot(_rms(c_q, qag_ref[...]).astype(BF16), wuq_ref[...], preferred_element_type=F32)
    qg = qg_ref[...]
    q_scale = QK_HEAD_DIM ** -0.5 * LOG2E
    for hd in range(MLA_HEADS):
        nope = qf[:, hd * HEAD_PAD: hd * HEAD_PAD + LANES]
        rt = qf[:, hd * HEAD_PAD + LANES: (hd + 1) * HEAD_PAD]
        ss = jnp.sum(nope * nope + 0.5 * (rt * rt), axis=-1, keepdims=True)
        r = lax.rsqrt(ss * inv_hd + EPS)
        q_ref[:, hd * HEAD_PAD: hd * HEAD_PAD + LANES] = (nope * r * qg[:, :LANES] * q_scale).astype(BF16)
        q_ref[:, hd * HEAD_PAD + LANES: (hd + 1) * HEAD_PAD] = (
            rope(rt * r * qg[:, LANES:]) * q_scale).astype(BF16)

    kvn = _rms(c_kv, kvag_ref[...]).astype(BF16)
    kf = jnp.dot(kvn, wuk_ref[...], preferred_element_type=F32)
    kg = kg_ref[...]
    kr_rot = rope(kr * kg[:, LANES:])
    ss_r = 0.5 * jnp.sum(kr * kr, axis=-1, keepdims=True)
    for hd in range(MLA_HEADS):
        kn = kf[:, hd * QK_NOPE_DIM: (hd + 1) * QK_NOPE_DIM]
        ss = jnp.sum(kn * kn, axis=-1, keepdims=True) + ss_r
        r = lax.rsqrt(ss * inv_hd + EPS)
        k_ref[:, hd * HEAD_PAD: hd * HEAD_PAD + LANES] = (kn * r * kg[:, :LANES]).astype(BF16)
        k_ref[:, hd * HEAD_PAD + LANES: (hd + 1) * HEAD_PAD] = (kr_rot * r).astype(BF16)
    vt_ref[0] = lax.dot_general(wvt_ref[...], kvn, (((1,), (1,)), ((), ())),
                                preferred_element_type=F32).astype(BF16)

    ug = jax.nn.gelu(u)
    vn = _rms(jax.nn.gelu(v), gvg_ref[...]).astype(BF16)
    nchunk = tm // CHUNK
    trow = lax.broadcasted_iota(jnp.int32, (CHUNK, CHUNK), 0)
    scol = lax.broadcasted_iota(jnp.int32, (CHUNK, CHUNK), 1)
    cols = []
    for g in range(GM_GROUPS):
        wc = jnp.where(scol <= trow, ws_ref[g], jnp.zeros((), BF16))
        lo, hi = g * GM_GROUP_DIM, (g + 1) * GM_GROUP_DIM
        rhs = jnp.concatenate([vn[c * CHUNK:(c + 1) * CHUNK, lo:hi] for c in range(nchunk)], axis=1)
        gate = jnp.dot(wc, rhs, preferred_element_type=F32) + bs_ref[:, g:g + 1]
        cols.append(jnp.concatenate(
            [gate[:, c * GM_GROUP_DIM:(c + 1) * GM_GROUP_DIM] for c in range(nchunk)], axis=0))
    g_out = ug * jnp.concatenate(cols, axis=1)
    gn_ref[...] = _rms(g_out, gog_ref[...]).astype(BF16)


def _mix_in(h, mixg, win, qag, wuq, kvag, wuk, wvt, qg, kg, gvg, ws, bs, gog, cos, sin, *, tm):
    t, d = h.shape
    zc = win.shape[1]
    qw, kw, vw, gw = MLA_HEADS * HEAD_PAD, MLA_HEADS * QK_NOPE_DIM, MLA_HEADS * V_HEAD_DIM, GM_GROUPS * GM_GROUP_DIM

    def const(shape):
        return pl.BlockSpec(shape, lambda i: (0,) * len(shape), pipeline_mode=pl.Buffered(1))

    def rows(width):
        return pl.BlockSpec((tm, width), lambda i: (i, 0))

    weights = (d * zc + Q_LORA_RANK * qw + KV_LORA_RANK * (kw + vw) + GM_GROUPS * CHUNK * CHUNK) * 2
    est = weights + 2 * tm * d * 4 + 2 * tm * (2 * qw + vw + gw) * 2 + 4 * tm * LANES * 4 + 3 * tm * zc * 4
    return pl.pallas_call(
        functools.partial(_mix_in_kernel, tm=tm),
        out_shape=(jax.ShapeDtypeStruct((t, qw), BF16), jax.ShapeDtypeStruct((t, qw), BF16),
                   jax.ShapeDtypeStruct((t // tm, vw, tm), BF16), jax.ShapeDtypeStruct((t, gw), BF16)),
        grid=(t // tm,),
        in_specs=[rows(d), const((1, d)), const((d, zc)), const((1, Q_LORA_RANK)), const((Q_LORA_RANK, qw)),
                  const((1, KV_LORA_RANK)), const((KV_LORA_RANK, kw)), const((vw, KV_LORA_RANK)),
                  const((1, HEAD_PAD)), const((1, HEAD_PAD)),
                  const((1, gw)), const((GM_GROUPS, CHUNK, CHUNK)), const((CHUNK, GM_GROUPS)), const((1, gw)),
                  rows(LANES), rows(LANES)],
        out_specs=(rows(qw), rows(qw), pl.BlockSpec((1, vw, tm), lambda i: (i, 0, 0)), rows(gw)),
        compiler_params=pltpu.CompilerParams(
            dimension_semantics=("parallel",), vmem_limit_bytes=_vmem_limit(est)),
        name="mix_in",
    )(h, mixg, win, qag, wuq, kvag, wuk, wvt, qg, kg, gvg, ws, bs, gog, cos, sin)


def _attn_kernel(q_ref, k_ref, vt_ref, o_ref, *acc_refs, tq, hg):
    i = pl.program_id(2)
    for acc_ref in acc_refs:
        acc_ref[...] = jnp.zeros(acc_ref.shape, F32)

    ones = jnp.ones((SUM_ROWS, tq), BF16)

    def step(j, ms, diagonal):
        start = pl.multiple_of(j * tq, tq)
        if diagonal:
            key = lax.broadcasted_iota(jnp.int32, (tq, tq), 0)
            qry = lax.broadcasted_iota(jnp.int32, (tq, tq), 1)
            keep = key <= qry
        sts = []
        for hd in range(hg):
            q = q_ref[0, :, hd * HEAD_PAD:(hd + 1) * HEAD_PAD]
            kj = k_ref[0, pl.ds(start, tq), hd * HEAD_PAD:(hd + 1) * HEAD_PAD]
            sts.append(lax.dot_general(kj, q, (((1,), (1,)), ((), ())), preferred_element_type=F32))
        new_ms, pts, alphas = [], [], []
        for hd in range(hg):
            st = jnp.where(keep, sts[hd], NEG_BIG) if diagonal else sts[hd]
            m_new = jnp.maximum(ms[hd], jnp.max(st, axis=0, keepdims=True))
            pts.append(jnp.exp2(st - m_new).astype(BF16))
            alphas.append(jnp.exp2(ms[hd] - m_new))
            new_ms.append(m_new)
        for hd in range(hg):
            vt1 = jnp.concatenate([vt_ref[j, hd * V_HEAD_DIM:(hd + 1) * V_HEAD_DIM, :], ones], axis=0)
            acc_refs[hd][...] = alphas[hd] * acc_refs[hd][...] + jnp.dot(
                vt1, pts[hd], preferred_element_type=F32)
        return tuple(new_ms)

    init = tuple(jnp.full((1, tq), NEG_BIG, F32) for _ in range(hg))
    ms = lax.fori_loop(0, i, lambda j, c: step(j, c, False), init)
    step(i, ms, True)
    for hd in range(hg):
        acc = acc_refs[hd][...]
        o_ref[0, :, hd * V_HEAD_DIM:(hd + 1) * V_HEAD_DIM] = (
            acc[:V_HEAD_DIM] / acc[V_HEAD_DIM:V_HEAD_DIM + 1]).T


def _attention(q, k, vt, *, tq, hg):
    b, s, _ = q.shape
    nkv = s // tq
    est = 2 * hg * tq * HEAD_PAD * 2 + 2 * hg * s * (HEAD_PAD + V_HEAD_DIM) * 2 + 2 * hg * tq * V_HEAD_DIM * 4 \
        + hg * tq * (V_HEAD_DIM + SUM_ROWS) * 4 + 2 * hg * tq * tq * 4
    return pl.pallas_call(
        functools.partial(_attn_kernel, tq=tq, hg=hg),
        out_shape=jax.ShapeDtypeStruct((b, s, MLA_HEADS * V_HEAD_DIM), F32),
        grid=(b, MLA_HEADS // hg, nkv),
        in_specs=[
            pl.BlockSpec((1, tq, hg * HEAD_PAD), lambda bi, hi, qi: (bi, qi, hi)),
            pl.BlockSpec((1, s, hg * HEAD_PAD), lambda bi, hi, qi: (bi, 0, hi)),
            pl.BlockSpec((nkv, hg * V_HEAD_DIM, tq), lambda bi, hi, qi: (bi, hi, 0)),
        ],
        out_specs=pl.BlockSpec((1, tq, hg * V_HEAD_DIM), lambda bi, hi, qi: (bi, qi, hi)),
        scratch_shapes=[pltpu.VMEM((V_HEAD_DIM + SUM_ROWS, tq), F32) for _ in range(hg)],
        compiler_params=pltpu.CompilerParams(
            dimension_semantics=("parallel", "parallel", "arbitrary"), vmem_limit_bytes=_vmem_limit(est)),
        name="attn",
    )(q, k, vt)


def _mix_out_kernel(h_ref, a_ref, gn_ref, ag_ref, wout_ref, o_ref):
    an = _rms(a_ref[...], ag_ref[...]).astype(BF16)
    mixed = jnp.concatenate([an, gn_ref[...]], axis=1)
    o_ref[...] = h_ref[...] + jnp.dot(mixed, wout_ref[...], preferred_element_type=F32)


def _mix_out(h, a, gn, ag, wout, *, tm):
    t, d = h.shape
    aw, gw = a.shape[1], gn.shape[1]
    est = wout.size * 2 + 2 * tm * (2 * d * 4 + aw * 4 + gw * 2) + 3 * tm * d * 4
    return pl.pallas_call(
        _mix_out_kernel,
        out_shape=jax.ShapeDtypeStruct((t, d), F32),
        grid=(t // tm,),
        in_specs=[
            pl.BlockSpec((tm, d), lambda i: (i, 0)),
            pl.BlockSpec((tm, aw), lambda i: (i, 0)),
            pl.BlockSpec((tm, gw), lambda i: (i, 0)),
            pl.BlockSpec((1, aw), lambda i: (0, 0), pipeline_mode=pl.Buffered(1)),
            pl.BlockSpec((aw + gw, d), lambda i: (0, 0), pipeline_mode=pl.Buffered(1)),
        ],
        out_specs=pl.BlockSpec((tm, d), lambda i: (i, 0)),
        compiler_params=pltpu.CompilerParams(
            dimension_semantics=("parallel",), vmem_limit_bytes=_vmem_limit(est)),
        name="mix_out",
    )(h, a, gn, ag, wout)


def _ple_kernel(h_ref, p_ref, gg_ref, wg_ref, wp_ref, pg_ref, o_ref):
    x = h_ref[...]
    e = _rms(jnp.dot(p_ref[...].astype(BF16), wp_ref[...], preferred_element_type=F32), pg_ref[...])
    gate = jax.nn.sigmoid(
        jnp.dot(_rms(x, gg_ref[...]).astype(BF16), wg_ref[...], preferred_element_type=F32))
    o_ref[...] = x + gate * e


def _ple(h, p, layer, gg, wg, wp, pg, *, tm):
    t, d = h.shape
    pd = p.shape[-1]
    est = (wg.size + wp.size) * 2 + 2 * tm * (2 * d * 4 + pd * 4) + 4 * tm * d * 4
    return pl.pallas_call(
        _ple_kernel,
        out_shape=jax.ShapeDtypeStruct((t, d), F32),
        grid=(t // tm,),
        in_specs=[
            pl.BlockSpec((tm, d), lambda i: (i, 0)),
            pl.BlockSpec((None, tm, pd), lambda i: (layer, i, 0)),
            pl.BlockSpec((1, d), lambda i: (0, 0), pipeline_mode=pl.Buffered(1)),
            pl.BlockSpec((d, d), lambda i: (0, 0), pipeline_mode=pl.Buffered(1)),
            pl.BlockSpec((pd, d), lambda i: (0, 0), pipeline_mode=pl.Buffered(1)),
            pl.BlockSpec((1, d), lambda i: (0, 0), pipeline_mode=pl.Buffered(1)),
        ],
        out_specs=pl.BlockSpec((tm, d), lambda i: (i, 0)),
        compiler_params=pltpu.CompilerParams(
            dimension_semantics=("parallel",), vmem_limit_bytes=_vmem_limit(est)),
        name="ple",
    )(h, p, gg, wg, wp, pg)


def _cast_pad_kernel(w_ref, o_ref, *, axis, size):
    x = w_ref[...].astype(BF16)
    if axis == 0:
        o_ref[:size, :] = x
        o_ref[size:, :] = jnp.zeros((o_ref.shape[0] - size, o_ref.shape[1]), BF16)
    else:
        o_ref[:, :size] = x
        o_ref[:, size:] = jnp.zeros((o_ref.shape[0], o_ref.shape[1] - size), BF16)


def _cast_pad(w, *, axis, padded, blk):
    nl, r, c = w.shape
    if axis == 0:
        in_blk, out_blk, out_shape, grid = (None, r, blk), (None, padded, blk), (nl, padded, c), (nl, c // blk)
        idx = lambda l, j: (l, 0, j)
    else:
        in_blk, out_blk, out_shape, grid = (None, blk, c), (None, blk, padded), (nl, r, padded), (nl, r // blk)
        idx = lambda l, j: (l, j, 0)
    in_bytes, out_bytes = (r, c)[axis] * blk * 4, padded * blk * 2
    est = 3 * in_bytes + 2 * out_bytes
    return pl.pallas_call(
        functools.partial(_cast_pad_kernel, axis=axis, size=(r, c)[axis]),
        out_shape=jax.ShapeDtypeStruct(out_shape, BF16),
        grid=grid,
        in_specs=[pl.BlockSpec(in_blk, idx)],
        out_specs=pl.BlockSpec(out_blk, idx),
        compiler_params=pltpu.CompilerParams(
            dimension_semantics=("parallel", "parallel"), vmem_limit_bytes=_vmem_limit(est)),
        name="cast_pad",
    )(w)


def _ffn_weights(w1, w3, w2, *, tf):
    f = w1.shape[2]
    fp = -(-f // tf) * tf
    return (_cast_pad(w1, axis=1, padded=fp, blk=256), _cast_pad(w3, axis=1, padded=fp, blk=256),
            _cast_pad(w2, axis=0, padded=fp, blk=256))


def _dup_rope(x, axis_len_nope):
    return jnp.concatenate([x, x[..., axis_len_nope:]], axis=-1)


def kernel(x, p, positions, ffn_a_norm, ffn_a_w1, ffn_a_w3, ffn_a_w2, mix_norm, w_in, q_a_norm, w_uq, kv_a_norm, w_ukv, q_norm, k_norm, gm_v_norm, gm_ws, gm_bs, attn_out_norm, gm_out_norm, w_out, ffn_b_norm, ffn_b_w1, ffn_b_w3, ffn_b_w2, ple_gate_norm, w_ple_gate, w_ple, ple_norm):
    b, s, d = x.shape
    depth = p.shape[0]
    t = b * s
    tm_ffn, tf = 1024, 512
    ffn_a = _ffn_weights(ffn_a_w1, ffn_a_w3, ffn_a_w2, tf=tf)
    ffn_b = _ffn_weights(ffn_b_w1, ffn_b_w3, ffn_b_w2, tf=tf)
    tm_out, tm_ple, tq, heads_per_step = 512, 512, 512, 4

    cos, sin = _rope_tables(positions.astype(F32).reshape(t, 1), tm=1024)
    p2 = p.reshape(depth, t, p.shape[-1])
    h = x.reshape(t, d)
    o_kr = Q_LORA_RANK + KV_LORA_RANK
    for i in range(depth):
        h = _ffn(h, ffn_a_norm[i][None, :], *ffn_a, i, tm=tm_ffn, tf=tf)

        win = w_in[i].astype(BF16)
        win = jnp.concatenate(
            [win[:, :o_kr + QK_ROPE_DIM], win[:, o_kr:o_kr + QK_ROPE_DIM], win[:, o_kr + QK_ROPE_DIM:]], axis=1)
        wuq = _dup_rope(w_uq[i].astype(BF16).reshape(Q_LORA_RANK, MLA_HEADS, QK_HEAD_DIM), QK_NOPE_DIM)
        wuq = wuq.reshape(Q_LORA_RANK, MLA_HEADS * HEAD_PAD)
        wukv = w_ukv[i].astype(BF16).reshape(KV_LORA_RANK, MLA_HEADS, QK_NOPE_DIM + V_HEAD_DIM)
        wuk = wukv[:, :, :QK_NOPE_DIM].reshape(KV_LORA_RANK, MLA_HEADS * QK_NOPE_DIM)
        wvt = wukv[:, :, QK_NOPE_DIM:].reshape(KV_LORA_RANK, MLA_HEADS * V_HEAD_DIM).T
        q, k, vt, gn = _mix_in(
            h, mix_norm[i][None, :], win, q_a_norm[i][None, :], wuq, kv_a_norm[i][None, :],
            wuk, wvt, _dup_rope(q_norm[i], QK_NOPE_DIM)[None, :],
            _dup_rope(k_norm[i], QK_NOPE_DIM)[None, :], gm_v_norm[i][None, :], gm_ws[i].astype(BF16),
            gm_bs[i].T, gm_out_norm[i][None, :], cos, sin, tm=tq)
        a = _attention(q.reshape(b, s, -1), k.reshape(b, s, -1), vt, tq=tq, hg=heads_per_step)
        h = _mix_out(h, a.reshape(t, -1), gn, attn_out_norm[i][None, :], w_out[i].astype(BF16), tm=tm_out)

        h = _ffn(h, ffn_b_norm[i][None, :], *ffn_b, i, tm=tm_ffn, tf=tf)

        h = _ple(h, p2, i, ple_gate_norm[i][None, :], w_ple_gate[i].astype(BF16), w_ple[i].astype(BF16),
                 ple_norm[i][None, :], tm=tm_ple)
    return h.reshape(b, s, d)
```

```python
import functools
import math

import jax
import jax.numpy as jnp
from jax import lax
from jax.experimental import pallas as pl
from jax.experimental.pallas import tpu as pltpu

F32 = jnp.float32
BF16 = jnp.bfloat16

MLA_HEADS = 8
QK_NOPE_DIM = 128
QK_ROPE_DIM = 64
QK_HEAD_DIM = QK_NOPE_DIM + QK_ROPE_DIM
V_HEAD_DIM = 128
Q_LORA_RANK = 512
KV_LORA_RANK = 256
GM_GROUPS = 8
GM_GROUP_DIM = 128
CHUNK = 128
ROPE_BASE = 10000.0
EPS = 1e-6

LANES = 128
V7X_VMEM_BYTES = 64 * 1024 * 1024
V7X_VMEM_LIMIT_CAP = 60000 * 1024

HEAD_PAD = 2 * LANES
NEG_BIG = -1e30
SUM_ROWS = 16
LOG2E = math.log2(math.e)


def _vmem_limit(nbytes):
    assert nbytes <= V7X_VMEM_LIMIT_CAP, nbytes
    return int(min(V7X_VMEM_LIMIT_CAP, max(nbytes, 16 * 1024 * 1024)))


def _rms(x, g):
    ms = jnp.mean(x * x, axis=-1, keepdims=True)
    return x * lax.rsqrt(ms + EPS) * g


def _rope_kernel(pos_ref, invf_ref, cmask_ref, smask_ref, cos_ref, sin_ref):
    ang = pos_ref[...] * invf_ref[...]
    cos_ref[...] = jnp.cos(ang) * cmask_ref[...]
    sin_ref[...] = jnp.sin(ang) * smask_ref[...]


def _rope_tables(pos_f, *, tm):
    t = pos_f.shape[0]
    half = QK_ROPE_DIM // 2
    inv_freq = ROPE_BASE ** (-jnp.arange(0, QK_ROPE_DIM, 2, dtype=F32) / QK_ROPE_DIM)
    invf = jnp.tile(inv_freq, LANES // half)[None, :]
    ones, zeros = jnp.ones((half,), F32), jnp.zeros((half,), F32)
    cmask = jnp.concatenate([ones, ones, zeros, zeros])[None, :]
    smask = jnp.concatenate([-ones, ones, zeros, zeros])[None, :]
    row = pl.BlockSpec((1, LANES), lambda i: (0, 0))
    tab = pl.BlockSpec((tm, LANES), lambda i: (i, 0))
    return pl.pallas_call(
        _rope_kernel,
        out_shape=(jax.ShapeDtypeStruct((t, LANES), F32), jax.ShapeDtypeStruct((t, LANES), F32)),
        grid=(t // tm,),
        in_specs=[pl.BlockSpec((tm, 1), lambda i: (i, 0)), row, row, row],
        out_specs=(tab, tab),
        compiler_params=pltpu.CompilerParams(dimension_semantics=("parallel",)),
        name="rope_tables",
    )(pos_f, invf, cmask, smask)


def _ffn_kernel(h_ref, g_ref, w1_ref, w3_ref, w2_ref, o_ref, n_ref):
    k = pl.program_id(1)

    def half_swiglu(n):
        a = jnp.dot(n, w1_ref[...], preferred_element_type=F32)
        b = jnp.dot(n, w3_ref[...], preferred_element_type=F32)
        mid = (0.5 * a) * jax.nn.sigmoid(a) * b
        return jnp.dot(mid.astype(BF16), w2_ref[...], preferred_element_type=F32)

    @pl.when(k == 0)
    def _():
        x = h_ref[...]
        n = _rms(x, g_ref[...]).astype(BF16)
        n_ref[...] = n
        o_ref[...] = x + half_swiglu(n)

    @pl.when(k > 0)
    def _():
        o_ref[...] += half_swiglu(n_ref[...])


def _ffn(h, g, w1, w3, w2, layer, *, tm, tf):
    t, d = h.shape
    f = w2.shape[1]
    est = 2 * tm * d * 4 + 2 * tm * d * 4 + tm * d * 2 + 2 * 3 * d * tf * 2 + 4 * tm * tf * 4
    return pl.pallas_call(
        _ffn_kernel,
        out_shape=jax.ShapeDtypeStruct((t, d), F32),
        grid=(t // tm, f // tf),
        in_specs=[
            pl.BlockSpec((tm, d), lambda i, k: (i, 0)),
            pl.BlockSpec((1, d), lambda i, k: (0, 0)),
            pl.BlockSpec((None, d, tf), lambda i, k: (layer, 0, k)),
            pl.BlockSpec((None, d, tf), lambda i, k: (layer, 0, k)),
            pl.BlockSpec((None, tf, d), lambda i, k: (layer, k, 0)),
        ],
        out_specs=pl.BlockSpec((tm, d), lambda i, k: (i, 0)),
        scratch_shapes=[pltpu.VMEM((tm, d), BF16)],
        compiler_params=pltpu.CompilerParams(
            dimension_semantics=("parallel", "arbitrary"), vmem_limit_bytes=_vmem_limit(est)),
        name="ffn",
    )(h, g, w1, w3, w2)


def _mix_in_kernel(h_ref, mixg_ref, win_ref, qag_ref, wuq_ref, kvag_ref, wuk_ref, wvt_ref, qg_ref, kg_ref,
                   gvg_ref, ws_ref, bs_ref, gog_ref, cos_ref, sin_ref,
                   q_ref, k_ref, vt_ref, gn_ref, *, tm):
    n = _rms(h_ref[...], mixg_ref[...]).astype(BF16)
    o_kv = Q_LORA_RANK
    o_kr = o_kv + KV_LORA_RANK
    o_u = o_kr + LANES
    o_v = o_u + GM_GROUPS * GM_GROUP_DIM
    zc = jnp.dot(n, win_ref[:, :o_u], preferred_element_type=F32)
    c_q, c_kv, kr = zc[:, :o_kv], zc[:, o_kv:o_kr], zc[:, o_kr:o_u]
    qf = jnp.dot(_rms(c_q, qag_ref[...]).astype(BF16), wuq_ref[...], preferred_element_type=F32)
    kvn = _rms(c_kv, kvag_ref[...]).astype(BF16)
    kf = jnp.dot(kvn, wuk_ref[...], preferred_element_type=F32)
    vt_ref[0] = lax.dot_general(wvt_ref[...], kvn, (((1,), (1,)), ((), ())),
                                preferred_element_type=F32).astype(BF16)
    v = jnp.dot(n, win_ref[:, o_v:], preferred_element_type=F32)
    u = jnp.dot(n, win_ref[:, o_u:o_v], preferred_element_type=F32)
    cos, sin = cos_ref[...], sin_ref[...]
    inv_hd = 1.0 / QK_HEAD_DIM

    def rope(x):
        return x * cos + pltpu.roll(x, QK_ROPE_DIM // 2, 1) * sin

    qg = qg_ref[...]
    q_scale = QK_HEAD_DIM ** -0.5 * LOG2E
    for hd in range(MLA_HEADS):
        nope = qf[:, hd * HEAD_PAD: hd * HEAD_PAD + LANES]
        rt = qf[:, hd * HEAD_PAD + LANES: (hd + 1) * HEAD_PAD]
        ss = jnp.sum(nope * nope + 0.5 * (rt * rt), axis=-1, keepdims=True)
        r = lax.rsqrt(ss * inv_hd + EPS) * q_scale
        q_ref[:, hd * HEAD_PAD: hd * HEAD_PAD + LANES] = (nope * r * qg[:, :LANES]).astype(BF16)
        q_ref[:, hd * HEAD_PAD + LANES: (hd + 1) * HEAD_PAD] = rope(rt * r * qg[:, LANES:]).astype(BF16)

    kg = kg_ref[...]
    kr_rot = rope(kr * kg[:, LANES:])
    ss_r = 0.5 * jnp.sum(kr * kr, axis=-1, keepdims=True)
    for hd in range(MLA_HEADS):
        kn = kf[:, hd * QK_NOPE_DIM: (hd + 1) * QK_NOPE_DIM]
        ss = jnp.sum(kn * kn, axis=-1, keepdims=True) + ss_r
        r = lax.rsqrt(ss * inv_hd + EPS)
        k_ref[:, hd * HEAD_PAD: hd * HEAD_PAD + LANES] = (kn * r * kg[:, :LANES]).astype(BF16)
        k_ref[:, hd * HEAD_PAD + LANES: (hd + 1) * HEAD_PAD] = (kr_rot * r).astype(BF16)

    ug = jax.nn.gelu(u)
    vn = _rms(jax.nn.gelu(v), gvg_ref[...]).astype(BF16)
    nchunk = tm // CHUNK
    trow = lax.broadcasted_iota(jnp.int32, (CHUNK, CHUNK), 0)
    scol = lax.broadcasted_iota(jnp.int32, (CHUNK, CHUNK), 1)
    cols = []
    for g in range(GM_GROUPS):
        wc = jnp.where(scol <= trow, ws_ref[g], jnp.zeros((), BF16))
        lo, hi = g * GM_GROUP_DIM, (g + 1) * GM_GROUP_DIM
        rhs = jnp.concatenate([vn[c * CHUNK:(c + 1) * CHUNK, lo:hi] for c in range(nchunk)], axis=1)
        gate = jnp.dot(wc, rhs, preferred_element_type=F32) + bs_ref[:, g:g + 1]
        cols.append(jnp.concatenate(
            [gate[:, c * GM_GROUP_DIM:(c + 1) * GM_GROUP_DIM] for c in range(nchunk)], axis=0))
    g_out = ug * jnp.concatenate(cols, axis=1)
    gn_ref[...] = _rms(g_out, gog_ref[...]).astype(BF16)


def _mix_in(h, mixg, win, qag, wuq, kvag, wuk, wvt, qg, kg, gvg, ws, bs, gog, cos, sin, *, tm):
    t, d = h.shape
    zc = win.shape[1]
    qw, kw, vw, gw = MLA_HEADS * HEAD_PAD, MLA_HEADS * QK_NOPE_DIM, MLA_HEADS * V_HEAD_DIM, GM_GROUPS * GM_GROUP_DIM

    def const(shape):
        return pl.BlockSpec(shape, lambda i: (0,) * len(shape), pipeline_mode=pl.Buffered(1))

    def rows(width):
        return pl.BlockSpec((tm, width), lambda i: (i, 0))

    weights = (d * zc + Q_LORA_RANK * qw + KV_LORA_RANK * (kw + vw) + GM_GROUPS * CHUNK * CHUNK) * 2
    est = weights + 2 * tm * d * 4 + 2 * tm * (2 * qw + vw + gw) * 2 + 4 * tm * LANES * 4 + 3 * tm * zc * 4
    return pl.pallas_call(
        functools.partial(_mix_in_kernel, tm=tm),
        out_shape=(jax.ShapeDtypeStruct((t, qw), BF16), jax.ShapeDtypeStruct((t, qw), BF16),
                   jax.ShapeDtypeStruct((t // tm, vw, tm), BF16), jax.ShapeDtypeStruct((t, gw), BF16)),
        grid=(t // tm,),
        in_specs=[rows(d), const((1, d)), const((d, zc)), const((1, Q_LORA_RANK)), const((Q_LORA_RANK, qw)),
                  const((1, KV_LORA_RANK)), const((KV_LORA_RANK, kw)), const((vw, KV_LORA_RANK)),
                  const((1, HEAD_PAD)), const((1, HEAD_PAD)),
                  const((1, gw)), const((GM_GROUPS, CHUNK, CHUNK)), const((CHUNK, GM_GROUPS)), const((1, gw)),
                  rows(LANES), rows(LANES)],
        out_specs=(rows(qw), rows(qw), pl.BlockSpec((1, vw, tm), lambda i: (i, 0, 0)), rows(gw)),
        compiler_params=pltpu.CompilerParams(
            dimension_semantics=("parallel",), vmem_limit_bytes=_vmem_limit(est)),
        name="mix_in",
    )(h, mixg, win, qag, wuq, kvag, wuk, wvt, qg, kg, gvg, ws, bs, gog, cos, sin)


def _attn_kernel(q_ref, k_ref, vt_ref, o_ref, *acc_refs, tq, hg):
    i = pl.program_id(2)
    for acc_ref in acc_refs:
        acc_ref[...] = jnp.zeros(acc_ref.shape, F32)

    ones = jnp.ones((SUM_ROWS, tq), BF16)

    def step(j, ms, diagonal):
        start = pl.multiple_of(j * tq, tq)
        if diagonal:
            key = lax.broadcasted_iota(jnp.int32, (tq, tq), 0)
            qry = lax.broadcasted_iota(jnp.int32, (tq, tq), 1)
            keep = key <= qry
        sts = []
        for hd in range(hg):
            q = q_ref[0, :, hd * HEAD_PAD:(hd + 1) * HEAD_PAD]
            kj = k_ref[0, pl.ds(start, tq), hd * HEAD_PAD:(hd + 1) * HEAD_PAD]
            sts.append(lax.dot_general(kj, q, (((1,), (1,)), ((), ())), preferred_element_type=F32))
        new_ms, pts, alphas = [], [], []
        for hd in range(hg):
            st = jnp.where(keep, sts[hd], NEG_BIG) if diagonal else sts[hd]
            m_new = jnp.maximum(ms[hd], jnp.max(st, axis=0, keepdims=True))
            pts.append(jnp.exp2(st - m_new).astype(BF16))
            alphas.append(jnp.exp2(ms[hd] - m_new))
            new_ms.append(m_new)
        for hd in range(hg):
            vt1 = jnp.concatenate([vt_ref[j, hd * V_HEAD_DIM:(hd + 1) * V_HEAD_DIM, :], ones], axis=0)
            acc_refs[hd][...] = alphas[hd] * acc_refs[hd][...] + jnp.dot(
                vt1, pts[hd], preferred_element_type=F32)
        return tuple(new_ms)

    init = tuple(jnp.full((1, tq), NEG_BIG, F32) for _ in range(hg))
    ms = lax.fori_loop(0, i, lambda j, c: step(j, c, False), init)
    step(i, ms, True)
    for hd in range(hg):
        acc = acc_refs[hd][...]
        o_ref[0, :, hd * V_HEAD_DIM:(hd + 1) * V_HEAD_DIM] = (
            acc[:V_HEAD_DIM] / acc[V_HEAD_DIM:V_HEAD_DIM + 1]).T


def _attention(q, k, vt, *, tq, hg):
    b, s, _ = q.shape
    nkv = s // tq
    est = 2 * hg * tq * HEAD_PAD * 2 + 2 * hg * s * (HEAD_PAD + V_HEAD_DIM) * 2 + 2 * hg * tq * V_HEAD_DIM * 4 \
        + hg * tq * (V_HEAD_DIM + SUM_ROWS) * 4 + 2 * hg * tq * tq * 4
    return pl.pallas_call(
        functools.partial(_attn_kernel, tq=tq, hg=hg),
        out_shape=jax.ShapeDtypeStruct((b, s, MLA_HEADS * V_HEAD_DIM), F32),
        grid=(b, MLA_HEADS // hg, nkv),
        in_specs=[
            pl.BlockSpec((1, tq, hg * HEAD_PAD), lambda bi, hi, qi: (bi, qi, hi)),
            pl.BlockSpec((1, s, hg * HEAD_PAD), lambda bi, hi, qi: (bi, 0, hi)),
            pl.BlockSpec((nkv, hg * V_HEAD_DIM, tq), lambda bi, hi, qi: (bi, hi, 0)),
        ],
        out_specs=pl.BlockSpec((1, tq, hg * V_HEAD_DIM), lambda bi, hi, qi: (bi, qi, hi)),
        scratch_shapes=[pltpu.VMEM((V_HEAD_DIM + SUM_ROWS, tq), F32) for _ in range(hg)],
        compiler_params=pltpu.CompilerParams(
            dimension_semantics=("parallel", "parallel", "arbitrary"), vmem_limit_bytes=_vmem_limit(est)),
        name="attn",
    )(q, k, vt)


def _mix_out_kernel(h_ref, a_ref, gn_ref, ag_ref, wout_ref, o_ref):
    an = _rms(a_ref[...], ag_ref[...]).astype(BF16)
    mixed = jnp.concatenate([an, gn_ref[...]], axis=1)
    o_ref[...] = h_ref[...] + jnp.dot(mixed, wout_ref[...], preferred_element_type=F32)


def _mix_out(h, a, gn, ag, wout, *, tm):
    t, d = h.shape
    aw, gw = a.shape[1], gn.shape[1]
    est = wout.size * 2 + 2 * tm * (2 * d * 4 + aw * 4 + gw * 2) + 3 * tm * d * 4
    return pl.pallas_call(
        _mix_out_kernel,
        out_shape=jax.ShapeDtypeStruct((t, d), F32),
        grid=(t // tm,),
        in_specs=[
            pl.BlockSpec((tm, d), lambda i: (i, 0)),
            pl.BlockSpec((tm, aw), lambda i: (i, 0)),
            pl.BlockSpec((tm, gw), lambda i: (i, 0)),
            pl.BlockSpec((1, aw), lambda i: (0, 0), pipeline_mode=pl.Buffered(1)),
            pl.BlockSpec((aw + gw, d), lambda i: (0, 0), pipeline_mode=pl.Buffered(1)),
        ],
        out_specs=pl.BlockSpec((tm, d), lambda i: (i, 0)),
        compiler_params=pltpu.CompilerParams(
            dimension_semantics=("parallel",), vmem_limit_bytes=_vmem_limit(est)),
        name="mix_out",
    )(h, a, gn, ag, wout)


def _ple_kernel(h_ref, p_ref, gg_ref, wg_ref, wp_ref, pg_ref, o_ref):
    x = h_ref[...]
    e = _rms(jnp.dot(p_ref[...].astype(BF16), wp_ref[...], preferred_element_type=F32), pg_ref[...])
    gate = jax.nn.sigmoid(
        jnp.dot(_rms(x, gg_ref[...]).astype(BF16), wg_ref[...], preferred_element_type=F32))
    o_ref[...] = x + gate * e


def _ple(h, p, layer, gg, wg, wp, pg, *, tm):
    t, d = h.shape
    pd = p.shape[-1]
    est = (wg.size + wp.size) * 2 + 2 * tm * (2 * d * 4 + pd * 4) + 4 * tm * d * 4
    return pl.pallas_call(
        _ple_kernel,
        out_shape=jax.ShapeDtypeStruct((t, d), F32),
        grid=(t // tm,),
        in_specs=[
            pl.BlockSpec((tm, d), lambda i: (i, 0)),
            pl.BlockSpec((None, tm, pd), lambda i: (layer, i, 0)),
            pl.BlockSpec((1, d), lambda i: (0, 0), pipeline_mode=pl.Buffered(1)),
            pl.BlockSpec((d, d), lambda i: (0, 0), pipeline_mode=pl.Buffered(1)),
            pl.BlockSpec((pd, d), lambda i: (0, 0), pipeline_mode=pl.Buffered(1)),
            pl.BlockSpec((1, d), lambda i: (0, 0), pipeline_mode=pl.Buffered(1)),
        ],
        out_specs=pl.BlockSpec((tm, d), lambda i: (i, 0)),
        compiler_params=pltpu.CompilerParams(
            dimension_semantics=("parallel",), vmem_limit_bytes=_vmem_limit(est)),
        name="ple",
    )(h, p, gg, wg, wp, pg)


def _cast_pad_kernel(w_ref, o_ref, *, axis, size):
    x = w_ref[...].astype(BF16)
    if axis == 0:
        o_ref[:size, :] = x
        o_ref[size:, :] = jnp.zeros((o_ref.shape[0] - size, o_ref.shape[1]), BF16)
    else:
        o_ref[:, :size] = x
        o_ref[:, size:] = jnp.zeros((o_ref.shape[0], o_ref.shape[1] - size), BF16)


def _cast_pad(w, *, axis, padded, blk):
    nl, r, c = w.shape
    if axis == 0:
        in_blk, out_blk, out_shape, grid = (None, r, blk), (None, padded, blk), (nl, padded, c), (nl, c // blk)
        idx = lambda l, j: (l, 0, j)
    else:
        in_blk, out_blk, out_shape, grid = (None, blk, c), (None, blk, padded), (nl, r, padded), (nl, r // blk)
        idx = lambda l, j: (l, j, 0)
    in_bytes, out_bytes = (r, c)[axis] * blk * 4, padded * blk * 2
    est = 3 * in_bytes + 2 * out_bytes
    return pl.pallas_call(
        functools.partial(_cast_pad_kernel, axis=axis, size=(r, c)[axis]),
        out_shape=jax.ShapeDtypeStruct(out_shape, BF16),
        grid=grid,
        in_specs=[pl.BlockSpec(in_blk, idx)],
        out_specs=pl.BlockSpec(out_blk, idx),
        compiler_params=pltpu.CompilerParams(
            dimension_semantics=("parallel", "parallel"), vmem_limit_bytes=_vmem_limit(est)),
        name="cast_pad",
    )(w)


def _ffn_weights(w1, w3, w2, *, tf):
    f = w1.shape[2]
    fp = -(-f // tf) * tf
    return (_cast_pad(w1, axis=1, padded=fp, blk=256), _cast_pad(w3, axis=1, padded=fp, blk=256),
            _cast_pad(w2, axis=0, padded=fp, blk=256))


def _dup_rope(x, axis_len_nope):
    return jnp.concatenate([x, x[..., axis_len_nope:]], axis=-1)


def kernel(x, p, positions, ffn_a_norm, ffn_a_w1, ffn_a_w3, ffn_a_w2, mix_norm, w_in, q_a_norm, w_uq, kv_a_norm, w_ukv, q_norm, k_norm, gm_v_norm, gm_ws, gm_bs, attn_out_norm, gm_out_norm, w_out, ffn_b_norm, ffn_b_w1, ffn_b_w3, ffn_b_w2, ple_gate_norm, w_ple_gate, w_ple, ple_norm):
    b, s, d = x.shape
    depth = p.shape[0]
    t = b * s
    tm_ffn, tf = 1024, 512
    ffn_a = _ffn_weights(ffn_a_w1, ffn_a_w3, ffn_a_w2, tf=tf)
    ffn_b = _ffn_weights(ffn_b_w1, ffn_b_w3, ffn_b_w2, tf=tf)
    tm_out, tm_ple, tq, heads_per_step = 512, 512, 512, 4

    cos, sin = _rope_tables(positions.astype(F32).reshape(t, 1), tm=1024)
    p2 = p.reshape(depth, t, p.shape[-1])
    h = x.reshape(t, d)
    o_kr = Q_LORA_RANK + KV_LORA_RANK
    for i in range(depth):
        h = _ffn(h, ffn_a_norm[i][None, :], *ffn_a, i, tm=tm_ffn, tf=tf)

        win = w_in[i].astype(BF16)
        win = jnp.concatenate(
            [win[:, :o_kr + QK_ROPE_DIM], win[:, o_kr:o_kr + QK_ROPE_DIM], win[:, o_kr + QK_ROPE_DIM:]], axis=1)
        wuq = _dup_rope(w_uq[i].astype(BF16).reshape(Q_LORA_RANK, MLA_HEADS, QK_HEAD_DIM), QK_NOPE_DIM)
        wuq = wuq.reshape(Q_LORA_RANK, MLA_HEADS * HEAD_PAD)
        wukv = w_ukv[i].astype(BF16).reshape(KV_LORA_RANK, MLA_HEADS, QK_NOPE_DIM + V_HEAD_DIM)
        wuk = wukv[:, :, :QK_NOPE_DIM].reshape(KV_LORA_RANK, MLA_HEADS * QK_NOPE_DIM)
        wvt = wukv[:, :, QK_NOPE_DIM:].reshape(KV_LORA_RANK, MLA_HEADS * V_HEAD_DIM).T
        q, k, vt, gn = _mix_in(
            h, mix_norm[i][None, :], win, q_a_norm[i][None, :], wuq, kv_a_norm[i][None, :],
            wuk, wvt, _dup_rope(q_norm[i], QK_NOPE_DIM)[None, :],
            _dup_rope(k_norm[i], QK_NOPE_DIM)[None, :], gm_v_norm[i][None, :], gm_ws[i].astype(BF16),
            gm_bs[i].T, gm_out_norm[i][None, :], cos, sin, tm=tq)
        a = _attention(q.reshape(b, s, -1), k.reshape(b, s, -1), vt, tq=tq, hg=heads_per_step)
        h = _mix_out(h, a.reshape(t, -1), gn, attn_out_norm[i][None, :], w_out[i].astype(BF16), tm=tm_out)

        h = _ffn(h, ffn_b_norm[i][None, :], *ffn_b, i, tm=tm_ffn, tf=tf)

        h = _ple(h, p2, i, ple_gate_norm[i][None, :], w_ple_gate[i].astype(BF16), w_ple[i].astype(BF16),
                 ple_norm[i][None, :], tm=tm_ple)
    return h.reshape(b, s, d)
```

```python
import functools
import math

import jax
import jax.numpy as jnp
from jax import lax
from jax.experimental import pallas as pl
from jax.experimental.pallas import tpu as pltpu

F32 = jnp.float32
BF16 = jnp.bfloat16

MLA_HEADS = 8
QK_NOPE_DIM = 128
QK_ROPE_DIM = 64
QK_HEAD_DIM = QK_NOPE_DIM + QK_ROPE_DIM
V_HEAD_DIM = 128
Q_LORA_RANK = 512
KV_LORA_RANK = 256
GM_GROUPS = 8
GM_GROUP_DIM = 128
CHUNK = 128
ROPE_BASE = 10000.0
EPS = 1e-6

LANES = 128
V7X_VMEM_BYTES = 64 * 1024 * 1024
V7X_VMEM_LIMIT_CAP = 60000 * 1024

HEAD_PAD = 2 * LANES
NEG_BIG = -1e30
SUM_ROWS = 16
LOG2E = math.log2(math.e)


def _vmem_limit(nbytes):
    assert nbytes <= V7X_VMEM_LIMIT_CAP, nbytes
    return int(min(V7X_VMEM_LIMIT_CAP, max(nbytes, 16 * 1024 * 1024)))


def _rms(x, g):
    ms = jnp.mean(x * x, axis=-1, keepdims=True)
    return x * lax.rsqrt(ms + EPS) * g


def _rope_kernel(pos_ref, invf_ref, cmask_ref, smask_ref, cos_ref, sin_ref):
    ang = pos_ref[...] * invf_ref[...]
    cos_ref[...] = jnp.cos(ang) * cmask_ref[...]
    sin_ref[...] = jnp.sin(ang) * smask_ref[...]


def _rope_tables(pos_f, *, tm):
    t = pos_f.shape[0]
    half = QK_ROPE_DIM // 2
    inv_freq = ROPE_BASE ** (-jnp.arange(0, QK_ROPE_DIM, 2, dtype=F32) / QK_ROPE_DIM)
    invf = jnp.tile(inv_freq, LANES // half)[None, :]
    ones, zeros = jnp.ones((half,), F32), jnp.zeros((half,), F32)
    cmask = jnp.concatenate([ones, ones, zeros, zeros])[None, :]
    smask = jnp.concatenate([-ones, ones, zeros, zeros])[None, :]
    row = pl.BlockSpec((1, LANES), lambda i: (0, 0))
    tab = pl.BlockSpec((tm, LANES), lambda i: (i, 0))
    return pl.pallas_call(
        _rope_kernel,
        out_shape=(jax.ShapeDtypeStruct((t, LANES), F32), jax.ShapeDtypeStruct((t, LANES), F32)),
        grid=(t // tm,),
        in_specs=[pl.BlockSpec((tm, 1), lambda i: (i, 0)), row, row, row],
        out_specs=(tab, tab),
        compiler_params=pltpu.CompilerParams(dimension_semantics=("parallel",)),
        name="rope_tables",
    )(pos_f, invf, cmask, smask)


def _ffn_kernel(h_ref, g_ref, w1_ref, w3_ref, w2_ref, o_ref, n_ref):
    k = pl.program_id(1)

    def half_swiglu(n):
        a = jnp.dot(n, w1_ref[...], preferred_element_type=F32)
        b = jnp.dot(n, w3_ref[...], preferred_element_type=F32)
        mid = (0.5 * a) * jax.nn.sigmoid(a) * b
        return jnp.dot(mid.astype(BF16), w2_ref[...], preferred_element_type=F32)

    @pl.when(k == 0)
    def _():
        x = h_ref[...]
        n = _rms(x, g_ref[...]).astype(BF16)
        n_ref[...] = n
        o_ref[...] = x + half_swiglu(n)

    @pl.when(k > 0)
    def _():
        o_ref[...] += half_swiglu(n_ref[...])


def _ffn(h, g, w1, w3, w2, layer, *, tm, tf):
    t, d = h.shape
    f = w2.shape[1]
    est = 2 * tm * d * 4 + 2 * tm * d * 4 + tm * d * 2 + 2 * 3 * d * tf * 2 + 4 * tm * tf * 4
    return pl.pallas_call(
        _ffn_kernel,
        out_shape=jax.ShapeDtypeStruct((t, d), F32),
        grid=(t // tm, f // tf),
        in_specs=[
            pl.BlockSpec((tm, d), lambda i, k: (i, 0)),
            pl.BlockSpec((1, d), lambda i, k: (0, 0)),
            pl.BlockSpec((None, d, tf), lambda i, k: (layer, 0, k)),
            pl.BlockSpec((None, d, tf), lambda i, k: (layer, 0, k)),
            pl.BlockSpec((None, tf, d), lambda i, k: (layer, k, 0)),
        ],
        out_specs=pl.BlockSpec((tm, d), lambda i, k: (i, 0)),
        scratch_shapes=[pltpu.VMEM((tm, d), BF16)],
        compiler_params=pltpu.CompilerParams(
            dimension_semantics=("parallel", "arbitrary"), vmem_limit_bytes=_vmem_limit(est)),
        name="ffn",
    )(h, g, w1, w3, w2)


def _mix_in_kernel(h_ref, mixg_ref, win_ref, qag_ref, wuq_ref, kvag_ref, wuk_ref, wvt_ref, qg_ref, kg_ref,
                   gvg_ref, ws_ref, bs_ref, gog_ref, cos_ref, sin_ref,
                   q_ref, k_ref, vt_ref, gn_ref, *, tm):
    n = _rms(h_ref[...], mixg_ref[...]).astype(BF16)
    o_kv = Q_LORA_RANK
    o_kr = o_kv + KV_LORA_RANK
    o_u = o_kr + LANES
    o_v = o_u + GM_GROUPS * GM_GROUP_DIM
    zc = jnp.dot(n, win_ref[:, :o_u], preferred_element_type=F32)
    c_q, c_kv, kr = zc[:, :o_kv], zc[:, o_kv:o_kr], zc[:, o_kr:o_u]
    qf = jnp.dot(_rms(c_q, qag_ref[...]).astype(BF16), wuq_ref[...], preferred_element_type=F32)
    kvn = _rms(c_kv, kvag_ref[...]).astype(BF16)
    kf = jnp.dot(kvn, wuk_ref[...], preferred_element_type=F32)
    vt_ref[0] = lax.dot_general(wvt_ref[...], kvn, (((1,), (1,)), ((), ())),
                                preferred_element_type=F32).astype(BF16)
    v = jnp.dot(n, win_ref[:, o_v:], preferred_element_type=F32)
    u = jnp.dot(n, win_ref[:, o_u:o_v], preferred_element_type=F32)
    cos, sin = cos_ref[...], sin_ref[...]
    inv_hd = 1.0 / QK_HEAD_DIM

    def rope(x):
        return x * cos + pltpu.roll(x, QK_ROPE_DIM // 2, 1) * sin

    qg = qg_ref[...]
    q_scale = QK_HEAD_DIM ** -0.5 * LOG2E
    for hd in range(MLA_HEADS):
        nope = qf[:, hd * HEAD_PAD: hd * HEAD_PAD + LANES]
        rt = qf[:, hd * HEAD_PAD + LANES: (hd + 1) * HEAD_PAD]
        ss = jnp.sum(nope * nope + 0.5 * (rt * rt), axis=-1, keepdims=True)
        r = lax.rsqrt(ss * inv_hd + EPS) * q_scale
        q_ref[:, hd * HEAD_PAD: hd * HEAD_PAD + LANES] = (nope * r * qg[:, :LANES]).astype(BF16)
        q_ref[:, hd * HEAD_PAD + LANES: (hd + 1) * HEAD_PAD] = rope(rt * r * qg[:, LANES:]).astype(BF16)

    kg = kg_ref[...]
    kr_rot = rope(kr * kg[:, LANES:])
    ss_r = 0.5 * jnp.sum(kr * kr, axis=-1, keepdims=True)
    for hd in range(MLA_HEADS):
        kn = kf[:, hd * QK_NOPE_DIM: (hd + 1) * QK_NOPE_DIM]
        ss = jnp.sum(kn * kn, axis=-1, keepdims=True) + ss_r
        r = lax.rsqrt(ss * inv_hd + EPS)
        k_ref[:, hd * HEAD_PAD: hd * HEAD_PAD + LANES] = (kn * r * kg[:, :LANES]).astype(BF16)
        k_ref[:, hd * HEAD_PAD + LANES: (hd + 1) * HEAD_PAD] = (kr_rot * r).astype(BF16)

    ug = jax.nn.gelu(u)
    vn = _rms(jax.nn.gelu(v), gvg_ref[...]).astype(BF16)
    nchunk = tm // CHUNK
    trow = lax.broadcasted_iota(jnp.int32, (CHUNK, CHUNK), 0)
    scol = lax.broadcasted_iota(jnp.int32, (CHUNK, CHUNK), 1)
    cols = []
    for g in range(GM_GROUPS):
        wc = jnp.where(scol <= trow, ws_ref[g], jnp.zeros((), BF16))
        lo, hi = g * GM_GROUP_DIM, (g + 1) * GM_GROUP_DIM
        rhs = jnp.concatenate([vn[c * CHUNK:(c + 1) * CHUNK, lo:hi] for c in range(nchunk)], axis=1)
        gate = jnp.dot(wc, rhs, preferred_element_type=F32) + bs_ref[:, g:g + 1]
        cols.append(jnp.concatenate(
            [gate[:, c * GM_GROUP_DIM:(c + 1) * GM_GROUP_DIM] for c in range(nchunk)], axis=0))
    g_out = ug * jnp.concatenate(cols, axis=1)
    gn_ref[...] = _rms(g_out, gog_ref[...]).astype(BF16)


def _mix_in(h, mixg, win, qag, wuq, kvag, wuk, wvt, qg, kg, gvg, ws, bs, gog, cos, sin, *, tm):
    t, d = h.shape
    zc = win.shape[1]
    qw, kw, vw, gw = MLA_HEADS * HEAD_PAD, MLA_HEADS * QK_NOPE_DIM, MLA_HEADS * V_HEAD_DIM, GM_GROUPS * GM_GROUP_DIM

    def const(shape):
        return pl.BlockSpec(shape, lambda i: (0,) * len(shape), pipeline_mode=pl.Buffered(1))

    def rows(width):
        return pl.BlockSpec((tm, width), lambda i: (i, 0))

    weights = (d * zc + Q_LORA_RANK * qw + KV_LORA_RANK * (kw + vw) + GM_GROUPS * CHUNK * CHUNK) * 2
    est = weights + 2 * tm * d * 4 + 2 * tm * (2 * qw + vw + gw) * 2 + 4 * tm * LANES * 4 + 3 * tm * zc * 4
    return pl.pallas_call(
        functools.partial(_mix_in_kernel, tm=tm),
        out_shape=(jax.ShapeDtypeStruct((t, qw), BF16), jax.ShapeDtypeStruct((t, qw), BF16),
                   jax.ShapeDtypeStruct((t // tm, vw, tm), BF16), jax.ShapeDtypeStruct((t, gw), BF16)),
        grid=(t // tm,),
        in_specs=[rows(d), const((1, d)), const((d, zc)), const((1, Q_LORA_RANK)), const((Q_LORA_RANK, qw)),
                  const((1, KV_LORA_RANK)), const((KV_LORA_RANK, kw)), const((vw, KV_LORA_RANK)),
                  const((1, HEAD_PAD)), const((1, HEAD_PAD)),
                  const((1, gw)), const((GM_GROUPS, CHUNK, CHUNK)), const((CHUNK, GM_GROUPS)), const((1, gw)),
                  rows(LANES), rows(LANES)],
        out_specs=(rows(qw), rows(qw), pl.BlockSpec((1, vw, tm), lambda i: (i, 0, 0)), rows(gw)),
        compiler_params=pltpu.CompilerParams(
            dimension_semantics=("parallel",), vmem_limit_bytes=_vmem_limit(est)),
        name="mix_in",
    )(h, mixg, win, qag, wuq, kvag, wuk, wvt, qg, kg, gvg, ws, bs, gog, cos, sin)


def _attn_kernel(q_ref, k_ref, vt_ref, o_ref, *acc_refs, tq, hg):
    i = pl.program_id(2)
    for acc_ref in acc_refs:
        acc_ref[...] = jnp.zeros(acc_ref.shape, F32)

    ones = jnp.ones((SUM_ROWS, tq), BF16)

    def step(j, ms, diagonal):
        start = pl.multiple_of(j * tq, tq)
        if diagonal:
            key = lax.broadcasted_iota(jnp.int32, (tq, tq), 0)
            qry = lax.broadcasted_iota(jnp.int32, (tq, tq), 1)
            keep = key <= qry
        sts = []
        for hd in range(hg):
            qt = q_ref[0, hd * HEAD_PAD:(hd + 1) * HEAD_PAD, :]
            kj = k_ref[0, pl.ds(start, tq), hd * HEAD_PAD:(hd + 1) * HEAD_PAD]
            sts.append(jnp.dot(kj, qt, preferred_element_type=F32))
        new_ms, pts, alphas = [], [], []
        for hd in range(hg):
            st = jnp.where(keep, sts[hd], NEG_BIG) if diagonal else sts[hd]
            m_new = jnp.maximum(ms[hd], jnp.max(st, axis=0, keepdims=True))
            pts.append(jnp.exp2(st - m_new).astype(BF16))
            alphas.append(jnp.exp2(ms[hd] - m_new))
            new_ms.append(m_new)
        for hd in range(hg):
            vt1 = jnp.concatenate([vt_ref[j, hd * V_HEAD_DIM:(hd + 1) * V_HEAD_DIM, :], ones], axis=0)
            acc_refs[hd][...] = alphas[hd] * acc_refs[hd][...] + jnp.dot(
                vt1, pts[hd], preferred_element_type=F32)
        return tuple(new_ms)

    init = tuple(jnp.full((1, tq), NEG_BIG, F32) for _ in range(hg))
    ms = lax.fori_loop(0, i, lambda j, c: step(j, c, False), init)
    step(i, ms, True)
    for hd in range(hg):
        acc = acc_refs[hd][...]
        o_ref[0, :, hd * V_HEAD_DIM:(hd + 1) * V_HEAD_DIM] = (
            acc[:V_HEAD_DIM] / acc[V_HEAD_DIM:V_HEAD_DIM + 1]).T


def _attention(q, k, vt, *, tq, hg):
    b, s, _ = k.shape
    nkv = s // tq
    est = 2 * hg * tq * HEAD_PAD * 2 + 2 * hg * s * (HEAD_PAD + V_HEAD_DIM) * 2 + 2 * hg * tq * V_HEAD_DIM * 4 \
        + hg * tq * (V_HEAD_DIM + SUM_ROWS) * 4 + 2 * hg * tq * tq * 4
    return pl.pallas_call(
        functools.partial(_attn_kernel, tq=tq, hg=hg),
        out_shape=jax.ShapeDtypeStruct((b, s, MLA_HEADS * V_HEAD_DIM), F32),
        grid=(b, MLA_HEADS // hg, nkv),
        in_specs=[
            pl.BlockSpec((1, hg * HEAD_PAD, tq), lambda bi, hi, qi: (bi, hi, qi)),
            pl.BlockSpec((1, s, hg * HEAD_PAD), lambda bi, hi, qi: (bi, 0, hi)),
            pl.BlockSpec((nkv, hg * V_HEAD_DIM, tq), lambda bi, hi, qi: (bi, hi, 0)),
        ],
        out_specs=pl.BlockSpec((1, tq, hg * V_HEAD_DIM), lambda bi, hi, qi: (bi, qi, hi)),
        scratch_shapes=[pltpu.VMEM((V_HEAD_DIM + SUM_ROWS, tq), F32) for _ in range(hg)],
        compiler_params=pltpu.CompilerParams(
            dimension_semantics=("parallel", "parallel", "arbitrary"), vmem_limit_bytes=_vmem_limit(est)),
        name="attn",
    )(q, k, vt)


def _mix_out_kernel(h_ref, a_ref, gn_ref, ag_ref, wout_ref, o_ref):
    an = _rms(a_ref[...], ag_ref[...]).astype(BF16)
    mixed = jnp.concatenate([an, gn_ref[...]], axis=1)
    o_ref[...] = h_ref[...] + jnp.dot(mixed, wout_ref[...], preferred_element_type=F32)


def _mix_out(h, a, gn, ag, wout, *, tm):
    t, d = h.shape
    aw, gw = a.shape[1], gn.shape[1]
    est = wout.size * 2 + 2 * tm * (2 * d * 4 + aw * 4 + gw * 2) + 3 * tm * d * 4
    return pl.pallas_call(
        _mix_out_kernel,
        out_shape=jax.ShapeDtypeStruct((t, d), F32),
        grid=(t // tm,),
        in_specs=[
            pl.BlockSpec((tm, d), lambda i: (i, 0)),
            pl.BlockSpec((tm, aw), lambda i: (i, 0)),
            pl.BlockSpec((tm, gw), lambda i: (i, 0)),
            pl.BlockSpec((1, aw), lambda i: (0, 0), pipeline_mode=pl.Buffered(1)),
            pl.BlockSpec((aw + gw, d), lambda i: (0, 0), pipeline_mode=pl.Buffered(1)),
        ],
        out_specs=pl.BlockSpec((tm, d), lambda i: (i, 0)),
        compiler_params=pltpu.CompilerParams(
            dimension_semantics=("parallel",), vmem_limit_bytes=_vmem_limit(est)),
        name="mix_out",
    )(h, a, gn, ag, wout)


def _ple_kernel(h_ref, p_ref, gg_ref, wg_ref, wp_ref, pg_ref, o_ref):
    x = h_ref[...]
    e = _rms(jnp.dot(p_ref[...].astype(BF16), wp_ref[...], preferred_element_type=F32), pg_ref[...])
    gate = jax.nn.sigmoid(
        jnp.dot(_rms(x, gg_ref[...]).astype(BF16), wg_ref[...], preferred_element_type=F32))
    o_ref[...] = x + gate * e


def _ple(h, p, layer, gg, wg, wp, pg, *, tm):
    t, d = h.shape
    pd = p.shape[-1]
    est = (wg.size + wp.size) * 2 + 2 * tm * (2 * d * 4 + pd * 4) + 4 * tm * d * 4
    return pl.pallas_call(
        _ple_kernel,
        out_shape=jax.ShapeDtypeStruct((t, d), F32),
        grid=(t // tm,),
        in_specs=[
            pl.BlockSpec((tm, d), lambda i: (i, 0)),
            pl.BlockSpec((None, tm, pd), lambda i: (layer, i, 0)),
            pl.BlockSpec((1, d), lambda i: (0, 0), pipeline_mode=pl.Buffered(1)),
            pl.BlockSpec((d, d), lambda i: (0, 0), pipeline_mode=pl.Buffered(1)),
            pl.BlockSpec((pd, d), lambda i: (0, 0), pipeline_mode=pl.Buffered(1)),
            pl.BlockSpec((1, d), lambda i: (0, 0), pipeline_mode=pl.Buffered(1)),
        ],
        out_specs=pl.BlockSpec((tm, d), lambda i: (i, 0)),
        compiler_params=pltpu.CompilerParams(
            dimension_semantics=("parallel",), vmem_limit_bytes=_vmem_limit(est)),
        name="ple",
    )(h, p, gg, wg, wp, pg)


def _cast_pad_kernel(w_ref, o_ref, *, axis, size):
    x = w_ref[...].astype(BF16)
    if axis == 0:
        o_ref[:size, :] = x
        o_ref[size:, :] = jnp.zeros((o_ref.shape[0] - size, o_ref.shape[1]), BF16)
    else:
        o_ref[:, :size] = x
        o_ref[:, size:] = jnp.zeros((o_ref.shape[0], o_ref.shape[1] - size), BF16)


def _cast_pad(w, *, axis, padded, blk):
    nl, r, c = w.shape
    if axis == 0:
        in_blk, out_blk, out_shape, grid = (None, r, blk), (None, padded, blk), (nl, padded, c), (nl, c // blk)
        idx = lambda l, j: (l, 0, j)
    else:
        in_blk, out_blk, out_shape, grid = (None, blk, c), (None, blk, padded), (nl, r, padded), (nl, r // blk)
        idx = lambda l, j: (l, j, 0)
    in_bytes, out_bytes = (r, c)[axis] * blk * 4, padded * blk * 2
    est = 3 * in_bytes + 2 * out_bytes
    return pl.pallas_call(
        functools.partial(_cast_pad_kernel, axis=axis, size=(r, c)[axis]),
        out_shape=jax.ShapeDtypeStruct(out_shape, BF16),
        grid=grid,
        in_specs=[pl.BlockSpec(in_blk, idx)],
        out_specs=pl.BlockSpec(out_blk, idx),
        compiler_params=pltpu.CompilerParams(
            dimension_semantics=("parallel", "parallel"), vmem_limit_bytes=_vmem_limit(est)),
        name="cast_pad",
    )(w)


def _ffn_weights(w1, w3, w2, *, tf):
    f = w1.shape[2]
    fp = -(-f // tf) * tf
    return (_cast_pad(w1, axis=1, padded=fp, blk=256), _cast_pad(w3, axis=1, padded=fp, blk=256),
            _cast_pad(w2, axis=0, padded=fp, blk=256))


def _dup_rope(x, axis_len_nope):
    return jnp.concatenate([x, x[..., axis_len_nope:]], axis=-1)


def kernel(x, p, positions, ffn_a_norm, ffn_a_w1, ffn_a_w3, ffn_a_w2, mix_norm, w_in, q_a_norm, w_uq, kv_a_norm, w_ukv, q_norm, k_norm, gm_v_norm, gm_ws, gm_bs, attn_out_norm, gm_out_norm, w_out, ffn_b_norm, ffn_b_w1, ffn_b_w3, ffn_b_w2, ple_gate_norm, w_ple_gate, w_ple, ple_norm):
    b, s, d = x.shape
    depth = p.shape[0]
    t = b * s
    tm_ffn, tf = 1024, 512
    ffn_a = _ffn_weights(ffn_a_w1, ffn_a_w3, ffn_a_w2, tf=tf)
    ffn_b = _ffn_weights(ffn_b_w1, ffn_b_w3, ffn_b_w2, tf=tf)
    tm_out, tm_ple, tq, heads_per_step = 512, 512, 512, 4

    cos, sin = _rope_tables(positions.astype(F32).reshape(t, 1), tm=1024)
    p2 = p.reshape(depth, t, p.shape[-1])
    h = x.reshape(t, d)
    o_kr = Q_LORA_RANK + KV_LORA_RANK
    for i in range(depth):
        h = _ffn(h, ffn_a_norm[i][None, :], *ffn_a, i, tm=tm_ffn, tf=tf)

        win = w_in[i].astype(BF16)
        win = jnp.concatenate(
            [win[:, :o_kr + QK_ROPE_DIM], win[:, o_kr:o_kr + QK_ROPE_DIM], win[:, o_kr + QK_ROPE_DIM:]], axis=1)
        wuq = _dup_rope(w_uq[i].astype(BF16).reshape(Q_LORA_RANK, MLA_HEADS, QK_HEAD_DIM), QK_NOPE_DIM)
        wuq = wuq.reshape(Q_LORA_RANK, MLA_HEADS * HEAD_PAD)
        wukv = w_ukv[i].astype(BF16).reshape(KV_LORA_RANK, MLA_HEADS, QK_NOPE_DIM + V_HEAD_DIM)
        wuk = wukv[:, :, :QK_NOPE_DIM].reshape(KV_LORA_RANK, MLA_HEADS * QK_NOPE_DIM)
        wvt = wukv[:, :, QK_NOPE_DIM:].reshape(KV_LORA_RANK, MLA_HEADS * V_HEAD_DIM).T
        q, k, vt, gn = _mix_in(
            h, mix_norm[i][None, :], win, q_a_norm[i][None, :], wuq, kv_a_norm[i][None, :],
            wuk, wvt, _dup_rope(q_norm[i], QK_NOPE_DIM)[None, :],
            _dup_rope(k_norm[i], QK_NOPE_DIM)[None, :], gm_v_norm[i][None, :], gm_ws[i].astype(BF16),
            gm_bs[i].T, gm_out_norm[i][None, :], cos, sin, tm=tq)
        qt = q.reshape(b, s, -1).transpose(0, 2, 1)
        a = _attention(qt, k.reshape(b, s, -1), vt, tq=tq, hg=heads_per_step)
        h = _mix_out(h, a.reshape(t, -1), gn, attn_out_norm[i][None, :], w_out[i].astype(BF16), tm=tm_out)

        h = _ffn(h, ffn_b_norm[i][None, :], *ffn_b, i, tm=tm_ffn, tf=tf)

        h = _ple(h, p2, i, ple_gate_norm[i][None, :], w_ple_gate[i].astype(BF16), w_ple[i].astype(BF16),
                 ple_norm[i][None, :], tm=tm_ple)
    return h.reshape(b, s, d)
```

```python
import functools
import math

import jax
import jax.numpy as jnp
from jax import lax
from jax.experimental import pallas as pl
from jax.experimental.pallas import tpu as pltpu

F32 = jnp.float32
BF16 = jnp.bfloat16

MLA_HEADS = 8
QK_NOPE_DIM = 128
QK_ROPE_DIM = 64
QK_HEAD_DIM = QK_NOPE_DIM + QK_ROPE_DIM
V_HEAD_DIM = 128
Q_LORA_RANK = 512
KV_LORA_RANK = 256
GM_GROUPS = 8
GM_GROUP_DIM = 128
CHUNK = 128
ROPE_BASE = 10000.0
EPS = 1e-6

LANES = 128
V7X_VMEM_BYTES = 64 * 1024 * 1024
V7X_VMEM_LIMIT_CAP = 60000 * 1024

HEAD_PAD = 2 * LANES
NEG_BIG = -1e30
SUM_ROWS = 16
LOG2E = math.log2(math.e)


def _vmem_limit(nbytes):
    assert nbytes <= V7X_VMEM_LIMIT_CAP, nbytes
    return int(min(V7X_VMEM_LIMIT_CAP, max(nbytes, 16 * 1024 * 1024)))


def _rms(x, g):
    ms = jnp.mean(x * x, axis=-1, keepdims=True)
    return x * lax.rsqrt(ms + EPS) * g


def _rope_kernel(pos_ref, invf_ref, cmask_ref, smask_ref, cos_ref, sin_ref):
    ang = pos_ref[...] * invf_ref[...]
    cos_ref[...] = jnp.cos(ang) * cmask_ref[...]
    sin_ref[...] = jnp.sin(ang) * smask_ref[...]


def _rope_tables(pos_f, *, tm):
    t = pos_f.shape[0]
    half = QK_ROPE_DIM // 2
    inv_freq = ROPE_BASE ** (-jnp.arange(0, QK_ROPE_DIM, 2, dtype=F32) / QK_ROPE_DIM)
    invf = jnp.tile(inv_freq, LANES // half)[None, :]
    ones, zeros = jnp.ones((half,), F32), jnp.zeros((half,), F32)
    cmask = jnp.concatenate([ones, ones, zeros, zeros])[None, :]
    smask = jnp.concatenate([-ones, ones, zeros, zeros])[None, :]
    row = pl.BlockSpec((1, LANES), lambda i: (0, 0))
    tab = pl.BlockSpec((tm, LANES), lambda i: (i, 0))
    return pl.pallas_call(
        _rope_kernel,
        out_shape=(jax.ShapeDtypeStruct((t, LANES), F32), jax.ShapeDtypeStruct((t, LANES), F32)),
        grid=(t // tm,),
        in_specs=[pl.BlockSpec((tm, 1), lambda i: (i, 0)), row, row, row],
        out_specs=(tab, tab),
        compiler_params=pltpu.CompilerParams(dimension_semantics=("parallel",)),
        name="rope_tables",
    )(pos_f, invf, cmask, smask)


def _ffn_kernel(h_ref, g_ref, w1_ref, w3_ref, w2_ref, o_ref, n_ref):
    k = pl.program_id(1)

    def half_swiglu(n):
        a = jnp.dot(n, w1_ref[...], preferred_element_type=F32)
        b = jnp.dot(n, w3_ref[...], preferred_element_type=F32)
        mid = (0.5 * a) * jax.nn.sigmoid(a) * b
        return jnp.dot(mid.astype(BF16), w2_ref[...], preferred_element_type=F32)

    @pl.when(k == 0)
    def _():
        x = h_ref[...]
        n = _rms(x, g_ref[...]).astype(BF16)
        n_ref[...] = n
        o_ref[...] = x + half_swiglu(n)

    @pl.when(k > 0)
    def _():
        o_ref[...] += half_swiglu(n_ref[...])


def _ffn(h, g, w1, w3, w2, layer, *, tm, tf):
    t, d = h.shape
    f = w2.shape[1]
    est = 2 * tm * d * 4 + 2 * tm * d * 4 + tm * d * 2 + 2 * 3 * d * tf * 2 + 4 * tm * tf * 4
    return pl.pallas_call(
        _ffn_kernel,
        out_shape=jax.ShapeDtypeStruct((t, d), F32),
        grid=(t // tm, f // tf),
        in_specs=[
            pl.BlockSpec((tm, d), lambda i, k: (i, 0)),
            pl.BlockSpec((1, d), lambda i, k: (0, 0)),
            pl.BlockSpec((None, d, tf), lambda i, k: (layer, 0, k)),
            pl.BlockSpec((None, d, tf), lambda i, k: (layer, 0, k)),
            pl.BlockSpec((None, tf, d), lambda i, k: (layer, k, 0)),
        ],
        out_specs=pl.BlockSpec((tm, d), lambda i, k: (i, 0)),
        scratch_shapes=[pltpu.VMEM((tm, d), BF16)],
        compiler_params=pltpu.CompilerParams(
            dimension_semantics=("parallel", "arbitrary"), vmem_limit_bytes=_vmem_limit(est)),
        name="ffn",
    )(h, g, w1, w3, w2)


def _mix_in_kernel(h_ref, mixg_ref, win_ref, qag_ref, wuq_ref, kvag_ref, wuk_ref, wvt_ref, qg_ref, kg_ref,
                   gvg_ref, ws_ref, bs_ref, gog_ref, cos_ref, sin_ref,
                   q_ref, k_ref, vt_ref, gn_ref, *, tm):
    n = _rms(h_ref[...], mixg_ref[...]).astype(BF16)
    o_kv = Q_LORA_RANK
    o_kr = o_kv + KV_LORA_RANK
    o_u = o_kr + LANES
    o_v = o_u + GM_GROUPS * GM_GROUP_DIM
    zc = jnp.dot(n, win_ref[:, :o_u], preferred_element_type=F32)
    c_q, c_kv, kr = zc[:, :o_kv], zc[:, o_kv:o_kr], zc[:, o_kr:o_u]
    qf = jnp.dot(_rms(c_q, qag_ref[...]).astype(BF16), wuq_ref[...], preferred_element_type=F32)
    kvn = _rms(c_kv, kvag_ref[...]).astype(BF16)
    kf = jnp.dot(kvn, wuk_ref[...], preferred_element_type=F32)
    vt_ref[0] = lax.dot_general(wvt_ref[...], kvn, (((1,), (1,)), ((), ())),
                                preferred_element_type=F32).astype(BF16)
    v = jnp.dot(n, win_ref[:, o_v:], preferred_element_type=F32)
    u = jnp.dot(n, win_ref[:, o_u:o_v], preferred_element_type=F32)
    cos, sin = cos_ref[...], sin_ref[...]
    inv_hd = 1.0 / QK_HEAD_DIM

    def rope(x):
        return x * cos + pltpu.roll(x, QK_ROPE_DIM // 2, 1) * sin

    qg = qg_ref[...]
    q_scale = QK_HEAD_DIM ** -0.5 * LOG2E
    for hd in range(MLA_HEADS):
        nope = qf[:, hd * HEAD_PAD: hd * HEAD_PAD + LANES]
        rt = qf[:, hd * HEAD_PAD + LANES: (hd + 1) * HEAD_PAD]
        ss = jnp.sum(nope * nope + 0.5 * (rt * rt), axis=-1, keepdims=True)
        r = lax.rsqrt(ss * inv_hd + EPS) * q_scale
        q_ref[:, hd * HEAD_PAD: hd * HEAD_PAD + LANES] = (nope * r * qg[:, :LANES]).astype(BF16)
        q_ref[:, hd * HEAD_PAD + LANES: (hd + 1) * HEAD_PAD] = rope(rt * r * qg[:, LANES:]).astype(BF16)

    kg = kg_ref[...]
    kr_rot = rope(kr * kg[:, LANES:])
    ss_r = 0.5 * jnp.sum(kr * kr, axis=-1, keepdims=True)
    for hd in range(MLA_HEADS):
        kn = kf[:, hd * QK_NOPE_DIM: (hd + 1) * QK_NOPE_DIM]
        ss = jnp.sum(kn * kn, axis=-1, keepdims=True) + ss_r
        r = lax.rsqrt(ss * inv_hd + EPS)
        k_ref[:, hd * HEAD_PAD: hd * HEAD_PAD + LANES] = (kn * r * kg[:, :LANES]).astype(BF16)
        k_ref[:, hd * HEAD_PAD + LANES: (hd + 1) * HEAD_PAD] = (kr_rot * r).astype(BF16)

    ug = jax.nn.gelu(u)
    vn = _rms(jax.nn.gelu(v), gvg_ref[...]).astype(BF16)
    nchunk = tm // CHUNK
    trow = lax.broadcasted_iota(jnp.int32, (CHUNK, CHUNK), 0)
    scol = lax.broadcasted_iota(jnp.int32, (CHUNK, CHUNK), 1)
    cols = []
    for g in range(GM_GROUPS):
        wc = jnp.where(scol <= trow, ws_ref[g], jnp.zeros((), BF16))
        lo, hi = g * GM_GROUP_DIM, (g + 1) * GM_GROUP_DIM
        rhs = jnp.concatenate([vn[c * CHUNK:(c + 1) * CHUNK, lo:hi] for c in range(nchunk)], axis=1)
        gate = jnp.dot(wc, rhs, preferred_element_type=F32) + bs_ref[:, g:g + 1]
        cols.append(jnp.concatenate(
            [gate[:, c * GM_GROUP_DIM:(c + 1) * GM_GROUP_DIM] for c in range(nchunk)], axis=0))
    g_out = ug * jnp.concatenate(cols, axis=1)
    gn_ref[...] = _rms(g_out, gog_ref[...]).astype(BF16)


def _mix_in(h, mixg, win, qag, wuq, kvag, wuk, wvt, qg, kg, gvg, ws, bs, gog, cos, sin, *, tm):
    t, d = h.shape
    zc = win.shape[1]
    qw, kw, vw, gw = MLA_HEADS * HEAD_PAD, MLA_HEADS * QK_NOPE_DIM, MLA_HEADS * V_HEAD_DIM, GM_GROUPS * GM_GROUP_DIM

    def const(shape):
        return pl.BlockSpec(shape, lambda i: (0,) * len(shape), pipeline_mode=pl.Buffered(1))

    def rows(width):
        return pl.BlockSpec((tm, width), lambda i: (i, 0))

    weights = (d * zc + Q_LORA_RANK * qw + KV_LORA_RANK * (kw + vw) + GM_GROUPS * CHUNK * CHUNK) * 2
    est = weights + 2 * tm * d * 4 + 2 * tm * (2 * qw + vw + gw) * 2 + 4 * tm * LANES * 4 + 3 * tm * zc * 4
    return pl.pallas_call(
        functools.partial(_mix_in_kernel, tm=tm),
        out_shape=(jax.ShapeDtypeStruct((t, qw), BF16), jax.ShapeDtypeStruct((t, qw), BF16),
                   jax.ShapeDtypeStruct((t // tm, vw, tm), BF16), jax.ShapeDtypeStruct((t, gw), BF16)),
        grid=(t // tm,),
        in_specs=[rows(d), const((1, d)), const((d, zc)), const((1, Q_LORA_RANK)), const((Q_LORA_RANK, qw)),
                  const((1, KV_LORA_RANK)), const((KV_LORA_RANK, kw)), const((vw, KV_LORA_RANK)),
                  const((1, HEAD_PAD)), const((1, HEAD_PAD)),
                  const((1, gw)), const((GM_GROUPS, CHUNK, CHUNK)), const((CHUNK, GM_GROUPS)), const((1, gw)),
                  rows(LANES), rows(LANES)],
        out_specs=(rows(qw), rows(qw), pl.BlockSpec((1, vw, tm), lambda i: (i, 0, 0)), rows(gw)),
        compiler_params=pltpu.CompilerParams(
            dimension_semantics=("parallel",), vmem_limit_bytes=_vmem_limit(est)),
        name="mix_in",
    )(h, mixg, win, qag, wuq, kvag, wuk, wvt, qg, kg, gvg, ws, bs, gog, cos, sin)


def _attn_kernel(q_ref, k_ref, vt_ref, o_ref, st_ref, *acc_refs, tq, hg):
    i = pl.program_id(2)
    for acc_ref in acc_refs:
        acc_ref[...] = jnp.zeros(acc_ref.shape, F32)

    ones = jnp.ones((SUM_ROWS, tq), BF16)

    def step(j, ms, diagonal):
        start = pl.multiple_of(j * tq, tq)
        if diagonal:
            key = lax.broadcasted_iota(jnp.int32, (tq, tq), 0)
            qry = lax.broadcasted_iota(jnp.int32, (tq, tq), 1)
            keep = key <= qry
        for hd in range(hg):
            q = q_ref[0, :, hd * HEAD_PAD:(hd + 1) * HEAD_PAD]
            kj = k_ref[0, pl.ds(start, tq), hd * HEAD_PAD:(hd + 1) * HEAD_PAD]
            st_ref[hd] = lax.dot_general(kj, q, (((1,), (1,)), ((), ())), preferred_element_type=F32)
        new_ms, pts, alphas = [], [], []
        for hd in range(hg):
            st = jnp.where(keep, st_ref[hd], NEG_BIG) if diagonal else st_ref[hd]
            m_new = jnp.maximum(ms[hd], jnp.max(st, axis=0, keepdims=True))
            pts.append(jnp.exp2(st - m_new).astype(BF16))
            alphas.append(jnp.exp2(ms[hd] - m_new))
            new_ms.append(m_new)
        for hd in range(hg):
            vt1 = jnp.concatenate([vt_ref[j, hd * V_HEAD_DIM:(hd + 1) * V_HEAD_DIM, :], ones], axis=0)
            acc_refs[hd][...] = alphas[hd] * acc_refs[hd][...] + jnp.dot(
                vt1, pts[hd], preferred_element_type=F32)
        return tuple(new_ms)

    def pair(jj, ms):
        return step(2 * jj + 1, step(2 * jj, ms, False), False)

    init = tuple(jnp.full((1, tq), NEG_BIG, F32) for _ in range(hg))
    ms = lax.fori_loop(0, lax.shift_right_logical(i, 1), pair, init)
    ms = lax.cond(i % 2 == 1, lambda m: step(i - 1, m, False), lambda m: m, ms)
    step(i, ms, True)
    for hd in range(hg):
        acc = acc_refs[hd][...]
        o_ref[0, :, hd * V_HEAD_DIM:(hd + 1) * V_HEAD_DIM] = (
            acc[:V_HEAD_DIM] / acc[V_HEAD_DIM:V_HEAD_DIM + 1]).T


def _attention(q, k, vt, *, tq, hg):
    b, s, _ = q.shape
    nkv = s // tq
    est = 2 * hg * tq * HEAD_PAD * 2 + 2 * hg * s * (HEAD_PAD + V_HEAD_DIM) * 2 + 2 * hg * tq * V_HEAD_DIM * 4 \
        + hg * tq * (V_HEAD_DIM + SUM_ROWS) * 4 + hg * tq * tq * 4 + 2 * hg * tq * tq * 4
    return pl.pallas_call(
        functools.partial(_attn_kernel, tq=tq, hg=hg),
        out_shape=jax.ShapeDtypeStruct((b, s, MLA_HEADS * V_HEAD_DIM), F32),
        grid=(b, MLA_HEADS // hg, nkv),
        in_specs=[
            pl.BlockSpec((1, tq, hg * HEAD_PAD), lambda bi, hi, qi: (bi, qi, hi)),
            pl.BlockSpec((1, s, hg * HEAD_PAD), lambda bi, hi, qi: (bi, 0, hi)),
            pl.BlockSpec((nkv, hg * V_HEAD_DIM, tq), lambda bi, hi, qi: (bi, hi, 0)),
        ],
        out_specs=pl.BlockSpec((1, tq, hg * V_HEAD_DIM), lambda bi, hi, qi: (bi, qi, hi)),
        scratch_shapes=[pltpu.VMEM((hg, tq, tq), F32)]
        + [pltpu.VMEM((V_HEAD_DIM + SUM_ROWS, tq), F32) for _ in range(hg)],
        compiler_params=pltpu.CompilerParams(
            dimension_semantics=("parallel", "parallel", "arbitrary"), vmem_limit_bytes=_vmem_limit(est)),
        name="attn",
    )(q, k, vt)


def _mix_out_kernel(h_ref, a_ref, gn_ref, ag_ref, wout_ref, o_ref):
    an = _rms(a_ref[...], ag_ref[...]).astype(BF16)
    mixed = jnp.concatenate([an, gn_ref[...]], axis=1)
    o_ref[...] = h_ref[...] + jnp.dot(mixed, wout_ref[...], preferred_element_type=F32)


def _mix_out(h, a, gn, ag, wout, *, tm):
    t, d = h.shape
    aw, gw = a.shape[1], gn.shape[1]
    est = wout.size * 2 + 2 * tm * (2 * d * 4 + aw * 4 + gw * 2) + 3 * tm * d * 4
    return pl.pallas_call(
        _mix_out_kernel,
        out_shape=jax.ShapeDtypeStruct((t, d), F32),
        grid=(t // tm,),
        in_specs=[
            pl.BlockSpec((tm, d), lambda i: (i, 0)),
            pl.BlockSpec((tm, aw), lambda i: (i, 0)),
            pl.BlockSpec((tm, gw), lambda i: (i, 0)),
            pl.BlockSpec((1, aw), lambda i: (0, 0), pipeline_mode=pl.Buffered(1)),
            pl.BlockSpec((aw + gw, d), lambda i: (0, 0), pipeline_mode=pl.Buffered(1)),
        ],
        out_specs=pl.BlockSpec((tm, d), lambda i: (i, 0)),
        compiler_params=pltpu.CompilerParams(
            dimension_semantics=("parallel",), vmem_limit_bytes=_vmem_limit(est)),
        name="mix_out",
    )(h, a, gn, ag, wout)


def _ple_kernel(h_ref, p_ref, gg_ref, wg_ref, wp_ref, pg_ref, o_ref):
    x = h_ref[...]
    e = _rms(jnp.dot(p_ref[...].astype(BF16), wp_ref[...], preferred_element_type=F32), pg_ref[...])
    gate = jax.nn.sigmoid(
        jnp.dot(_rms(x, gg_ref[...]).astype(BF16), wg_ref[...], preferred_element_type=F32))
    o_ref[...] = x + gate * e


def _ple(h, p, layer, gg, wg, wp, pg, *, tm):
    t, d = h.shape
    pd = p.shape[-1]
    est = (wg.size + wp.size) * 2 + 2 * tm * (2 * d * 4 + pd * 4) + 4 * tm * d * 4
    return pl.pallas_call(
        _ple_kernel,
        out_shape=jax.ShapeDtypeStruct((t, d), F32),
        grid=(t // tm,),
        in_specs=[
            pl.BlockSpec((tm, d), lambda i: (i, 0)),
            pl.BlockSpec((None, tm, pd), lambda i: (layer, i, 0)),
            pl.BlockSpec((1, d), lambda i: (0, 0), pipeline_mode=pl.Buffered(1)),
            pl.BlockSpec((d, d), lambda i: (0, 0), pipeline_mode=pl.Buffered(1)),
            pl.BlockSpec((pd, d), lambda i: (0, 0), pipeline_mode=pl.Buffered(1)),
            pl.BlockSpec((1, d), lambda i: (0, 0), pipeline_mode=pl.Buffered(1)),
        ],
        out_specs=pl.BlockSpec((tm, d), lambda i: (i, 0)),
        compiler_params=pltpu.CompilerParams(
            dimension_semantics=("parallel",), vmem_limit_bytes=_vmem_limit(est)),
        name="ple",
    )(h, p, gg, wg, wp, pg)


def _cast_pad_kernel(w_ref, o_ref, *, axis, size):
    x = w_ref[...].astype(BF16)
    if axis == 0:
        o_ref[:size, :] = x
        o_ref[size:, :] = jnp.zeros((o_ref.shape[0] - size, o_ref.shape[1]), BF16)
    else:
        o_ref[:, :size] = x
        o_ref[:, size:] = jnp.zeros((o_ref.shape[0], o_ref.shape[1] - size), BF16)


def _cast_pad(w, *, axis, padded, blk):
    nl, r, c = w.shape
    if axis == 0:
        in_blk, out_blk, out_shape, grid = (None, r, blk), (None, padded, blk), (nl, padded, c), (nl, c // blk)
        idx = lambda l, j: (l, 0, j)
    else:
        in_blk, out_blk, out_shape, grid = (None, blk, c), (None, blk, padded), (nl, r, padded), (nl, r // blk)
        idx = lambda l, j: (l, j, 0)
    in_bytes, out_bytes = (r, c)[axis] * blk * 4, padded * blk * 2
    est = 3 * in_bytes + 2 * out_bytes
    return pl.pallas_call(
        functools.partial(_cast_pad_kernel, axis=axis, size=(r, c)[axis]),
        out_shape=jax.ShapeDtypeStruct(out_shape, BF16),
        grid=grid,
        in_specs=[pl.BlockSpec(in_blk, idx)],
        out_specs=pl.BlockSpec(out_blk, idx),
        compiler_params=pltpu.CompilerParams(
            dimension_semantics=("parallel", "parallel"), vmem_limit_bytes=_vmem_limit(est)),
        name="cast_pad",
    )(w)


def _ffn_weights(w1, w3, w2, *, tf):
    f = w1.shape[2]
    fp = -(-f // tf) * tf
    return (_cast_pad(w1, axis=1, padded=fp, blk=256), _cast_pad(w3, axis=1, padded=fp, blk=256),
            _cast_pad(w2, axis=0, padded=fp, blk=256))


def _dup_rope(x, axis_len_nope):
    return jnp.concatenate([x, x[..., axis_len_nope:]], axis=-1)


def kernel(x, p, positions, ffn_a_norm, ffn_a_w1, ffn_a_w3, ffn_a_w2, mix_norm, w_in, q_a_norm, w_uq, kv_a_norm, w_ukv, q_norm, k_norm, gm_v_norm, gm_ws, gm_bs, attn_out_norm, gm_out_norm, w_out, ffn_b_norm, ffn_b_w1, ffn_b_w3, ffn_b_w2, ple_gate_norm, w_ple_gate, w_ple, ple_norm):
    b, s, d = x.shape
    depth = p.shape[0]
    t = b * s
    tm_ffn, tf = 1024, 512
    ffn_a = _ffn_weights(ffn_a_w1, ffn_a_w3, ffn_a_w2, tf=tf)
    ffn_b = _ffn_weights(ffn_b_w1, ffn_b_w3, ffn_b_w2, tf=tf)
    tm_out, tm_ple, tq, heads_per_step = 512, 512, 512, 4

    cos, sin = _rope_tables(positions.astype(F32).reshape(t, 1), tm=1024)
    p2 = p.reshape(depth, t, p.shape[-1])
    h = x.reshape(t, d)
    o_kr = Q_LORA_RANK + KV_LORA_RANK
    for i in range(depth):
        h = _ffn(h, ffn_a_norm[i][None, :], *ffn_a, i, tm=tm_ffn, tf=tf)

        win = w_in[i].astype(BF16)
        win = jnp.concatenate(
            [win[:, :o_kr + QK_ROPE_DIM], win[:, o_kr:o_kr + QK_ROPE_DIM], win[:, o_kr + QK_ROPE_DIM:]], axis=1)
        wuq = _dup_rope(w_uq[i].astype(BF16).reshape(Q_LORA_RANK, MLA_HEADS, QK_HEAD_DIM), QK_NOPE_DIM)
        wuq = wuq.reshape(Q_LORA_RANK, MLA_HEADS * HEAD_PAD)
        wukv = w_ukv[i].astype(BF16).reshape(KV_LORA_RANK, MLA_HEADS, QK_NOPE_DIM + V_HEAD_DIM)
        wuk = wukv[:, :, :QK_NOPE_DIM].reshape(KV_LORA_RANK, MLA_HEADS * QK_NOPE_DIM)
        wvt = wukv[:, :, QK_NOPE_DIM:].reshape(KV_LORA_RANK, MLA_HEADS * V_HEAD_DIM).T
        q, k, vt, gn = _mix_in(
            h, mix_norm[i][None, :], win, q_a_norm[i][None, :], wuq, kv_a_norm[i][None, :],
            wuk, wvt, _dup_rope(q_norm[i], QK_NOPE_DIM)[None, :],
            _dup_rope(k_norm[i], QK_NOPE_DIM)[None, :], gm_v_norm[i][None, :], gm_ws[i].astype(BF16),
            gm_bs[i].T, gm_out_norm[i][None, :], cos, sin, tm=tq)
        a = _attention(q.reshape(b, s, -1), k.reshape(b, s, -1), vt, tq=tq, hg=heads_per_step)
        h = _mix_out(h, a.reshape(t, -1), gn, attn_out_norm[i][None, :], w_out[i].astype(BF16), tm=tm_out)

        h = _ffn(h, ffn_b_norm[i][None, :], *ffn_b, i, tm=tm_ffn, tf=tf)

        h = _ple(h, p2, i, ple_gate_norm[i][None, :], w_ple_gate[i].astype(BF16), w_ple[i].astype(BF16),
                 ple_norm[i][None, :], tm=tm_ple)
    return h.reshape(b, s, d)
```

```python
import functools
import math

import jax
import jax.numpy as jnp
from jax import lax
from jax.experimental import pallas as pl
from jax.experimental.pallas import tpu as pltpu

F32 = jnp.float32
BF16 = jnp.bfloat16

MLA_HEADS = 8
QK_NOPE_DIM = 128
QK_ROPE_DIM = 64
QK_HEAD_DIM = QK_NOPE_DIM + QK_ROPE_DIM
V_HEAD_DIM = 128
Q_LORA_RANK = 512
KV_LORA_RANK = 256
GM_GROUPS = 8
GM_GROUP_DIM = 128
CHUNK = 128
ROPE_BASE = 10000.0
EPS = 1e-6

LANES = 128
V7X_VMEM_BYTES = 64 * 1024 * 1024
V7X_VMEM_LIMIT_CAP = 60000 * 1024

HEAD_PAD = 2 * LANES
NEG_BIG = -1e30
SUM_ROWS = 16
LOG2E = math.log2(math.e)


def _vmem_limit(nbytes):
    assert nbytes <= V7X_VMEM_LIMIT_CAP, nbytes
    return int(min(V7X_VMEM_LIMIT_CAP, max(nbytes, 16 * 1024 * 1024)))


def _rms(x, g):
    ms = jnp.mean(x * x, axis=-1, keepdims=True)
    return x * lax.rsqrt(ms + EPS) * g


def _rope_kernel(pos_ref, invf_ref, cmask_ref, smask_ref, cos_ref, sin_ref):
    ang = pos_ref[...] * invf_ref[...]
    cos_ref[...] = jnp.cos(ang) * cmask_ref[...]
    sin_ref[...] = jnp.sin(ang) * smask_ref[...]


def _rope_tables(pos_f, *, tm):
    t = pos_f.shape[0]
    half = QK_ROPE_DIM // 2
    inv_freq = ROPE_BASE ** (-jnp.arange(0, QK_ROPE_DIM, 2, dtype=F32) / QK_ROPE_DIM)
    invf = jnp.tile(inv_freq, LANES // half)[None, :]
    ones, zeros = jnp.ones((half,), F32), jnp.zeros((half,), F32)
    cmask = jnp.concatenate([ones, ones, zeros, zeros])[None, :]
    smask = jnp.concatenate([-ones, ones, zeros, zeros])[None, :]
    row = pl.BlockSpec((1, LANES), lambda i: (0, 0))
    tab = pl.BlockSpec((tm, LANES), lambda i: (i, 0))
    return pl.pallas_call(
        _rope_kernel,
        out_shape=(jax.ShapeDtypeStruct((t, LANES), F32), jax.ShapeDtypeStruct((t, LANES), F32)),
        grid=(t // tm,),
        in_specs=[pl.BlockSpec((tm, 1), lambda i: (i, 0)), row, row, row],
        out_specs=(tab, tab),
        compiler_params=pltpu.CompilerParams(dimension_semantics=("parallel",)),
        name="rope_tables",
    )(pos_f, invf, cmask, smask)


def _ffn_kernel(h_ref, g_ref, w1_ref, w3_ref, w2_ref, o_ref, n_ref):
    k = pl.program_id(1)

    def half_swiglu(n):
        a = jnp.dot(n, w1_ref[...], preferred_element_type=F32)
        b = jnp.dot(n, w3_ref[...], preferred_element_type=F32)
        mid = (0.5 * a) * jax.nn.sigmoid(a) * b
        return jnp.dot(mid.astype(BF16), w2_ref[...], preferred_element_type=F32)

    @pl.when(k == 0)
    def _():
        x = h_ref[...]
        n = _rms(x, g_ref[...]).astype(BF16)
        n_ref[...] = n
        o_ref[...] = x + half_swiglu(n)

    @pl.when(k > 0)
    def _():
        o_ref[...] += half_swiglu(n_ref[...])


def _ffn(h, g, w1, w3, w2, layer, *, tm, tf):
    t, d = h.shape
    f = w2.shape[1]
    est = 2 * tm * d * 4 + 2 * tm * d * 4 + tm * d * 2 + 2 * 3 * d * tf * 2 + 4 * tm * tf * 4
    return pl.pallas_call(
        _ffn_kernel,
        out_shape=jax.ShapeDtypeStruct((t, d), F32),
        grid=(t // tm, f // tf),
        in_specs=[
            pl.BlockSpec((tm, d), lambda i, k: (i, 0)),
            pl.BlockSpec((1, d), lambda i, k: (0, 0)),
            pl.BlockSpec((None, d, tf), lambda i, k: (layer, 0, k)),
            pl.BlockSpec((None, d, tf), lambda i, k: (layer, 0, k)),
            pl.BlockSpec((None, tf, d), lambda i, k: (layer, k, 0)),
        ],
        out_specs=pl.BlockSpec((tm, d), lambda i, k: (i, 0)),
        scratch_shapes=[pltpu.VMEM((tm, d), BF16)],
        compiler_params=pltpu.CompilerParams(
            dimension_semantics=("parallel", "arbitrary"), vmem_limit_bytes=_vmem_limit(est)),
        name="ffn",
    )(h, g, w1, w3, w2)


def _mix_in_kernel(h_ref, mixg_ref, win_ref, qag_ref, wuq_ref, kvag_ref, wuk_ref, wvt_ref, qg_ref, kg_ref,
                   gvg_ref, ws_ref, bs_ref, gog_ref, cos_ref, sin_ref,
                   q_ref, k_ref, vt_ref, gn_ref, *, tm):
    n = _rms(h_ref[...], mixg_ref[...]).astype(BF16)
    o_kv = Q_LORA_RANK
    o_kr = o_kv + KV_LORA_RANK
    o_u = o_kr + LANES
    o_v = o_u + GM_GROUPS * GM_GROUP_DIM
    zc = jnp.dot(n, win_ref[:, :o_u], preferred_element_type=F32)
    c_q, c_kv, kr = zc[:, :o_kv], zc[:, o_kv:o_kr], zc[:, o_kr:o_u]
    qf = jnp.dot(_rms(c_q, qag_ref[...]).astype(BF16), wuq_ref[...], preferred_element_type=F32)
    kvn = _rms(c_kv, kvag_ref[...]).astype(BF16)
    kf = jnp.dot(kvn, wuk_ref[...], preferred_element_type=F32)
    vt_ref[0] = lax.dot_general(wvt_ref[...], kvn, (((1,), (1,)), ((), ())),
                                preferred_element_type=F32).astype(BF16)
    v = jnp.dot(n, win_ref[:, o_v:], preferred_element_type=F32)
    u = jnp.dot(n, win_ref[:, o_u:o_v], preferred_element_type=F32)
    cos, sin = cos_ref[...], sin_ref[...]
    inv_hd = 1.0 / QK_HEAD_DIM

    def rope(x):
        return x * cos + pltpu.roll(x, QK_ROPE_DIM // 2, 1) * sin

    qg = qg_ref[...]
    q_scale = QK_HEAD_DIM ** -0.5 * LOG2E
    for hd in range(MLA_HEADS):
        nope = qf[:, hd * HEAD_PAD: hd * HEAD_PAD + LANES]
        rt = qf[:, hd * HEAD_PAD + LANES: (hd + 1) * HEAD_PAD]
        ss = jnp.sum(nope * nope + 0.5 * (rt * rt), axis=-1, keepdims=True)
        r = lax.rsqrt(ss * inv_hd + EPS) * q_scale
        q_ref[:, hd * HEAD_PAD: hd * HEAD_PAD + LANES] = (nope * r * qg[:, :LANES]).astype(BF16)
        q_ref[:, hd * HEAD_PAD + LANES: (hd + 1) * HEAD_PAD] = rope(rt * r * qg[:, LANES:]).astype(BF16)

    kg = kg_ref[...]
    kr_rot = rope(kr * kg[:, LANES:])
    ss_r = 0.5 * jnp.sum(kr * kr, axis=-1, keepdims=True)
    for hd in range(MLA_HEADS):
        kn = kf[:, hd * QK_NOPE_DIM: (hd + 1) * QK_NOPE_DIM]
        ss = jnp.sum(kn * kn, axis=-1, keepdims=True) + ss_r
        r = lax.rsqrt(ss * inv_hd + EPS)
        k_ref[:, hd * HEAD_PAD: hd * HEAD_PAD + LANES] = (kn * r * kg[:, :LANES]).astype(BF16)
        k_ref[:, hd * HEAD_PAD + LANES: (hd + 1) * HEAD_PAD] = (kr_rot * r).astype(BF16)

    ug = jax.nn.gelu(u)
    vn = _rms(jax.nn.gelu(v), gvg_ref[...]).astype(BF16)
    nchunk = tm // CHUNK
    trow = lax.broadcasted_iota(jnp.int32, (CHUNK, CHUNK), 0)
    scol = lax.broadcasted_iota(jnp.int32, (CHUNK, CHUNK), 1)
    cols = []
    for g in range(GM_GROUPS):
        wc = jnp.where(scol <= trow, ws_ref[g], jnp.zeros((), BF16))
        lo, hi = g * GM_GROUP_DIM, (g + 1) * GM_GROUP_DIM
        rhs = jnp.concatenate([vn[c * CHUNK:(c + 1) * CHUNK, lo:hi] for c in range(nchunk)], axis=1)
        gate = jnp.dot(wc, rhs, preferred_element_type=F32) + bs_ref[:, g:g + 1]
        cols.append(jnp.concatenate(
            [gate[:, c * GM_GROUP_DIM:(c + 1) * GM_GROUP_DIM] for c in range(nchunk)], axis=0))
    g_out = ug * jnp.concatenate(cols, axis=1)
    gn_ref[...] = _rms(g_out, gog_ref[...]).astype(BF16)


def _mix_in(h, mixg, win, qag, wuq, kvag, wuk, wvt, qg, kg, gvg, ws, bs, gog, cos, sin, *, tm):
    t, d = h.shape
    zc = win.shape[1]
    qw, kw, vw, gw = MLA_HEADS * HEAD_PAD, MLA_HEADS * QK_NOPE_DIM, MLA_HEADS * V_HEAD_DIM, GM_GROUPS * GM_GROUP_DIM

    def const(shape):
        return pl.BlockSpec(shape, lambda i: (0,) * len(shape), pipeline_mode=pl.Buffered(1))

    def rows(width):
        return pl.BlockSpec((tm, width), lambda i: (i, 0))

    weights = (d * zc + Q_LORA_RANK * qw + KV_LORA_RANK * (kw + vw) + GM_GROUPS * CHUNK * CHUNK) * 2
    est = weights + 2 * tm * d * 4 + 2 * tm * (2 * qw + vw + gw) * 2 + 4 * tm * LANES * 4 + 3 * tm * zc * 4
    return pl.pallas_call(
        functools.partial(_mix_in_kernel, tm=tm),
        out_shape=(jax.ShapeDtypeStruct((t, qw), BF16), jax.ShapeDtypeStruct((t, qw), BF16),
                   jax.ShapeDtypeStruct((t // tm, vw, tm), BF16), jax.ShapeDtypeStruct((t, gw), BF16)),
        grid=(t // tm,),
        in_specs=[rows(d), const((1, d)), const((d, zc)), const((1, Q_LORA_RANK)), const((Q_LORA_RANK, qw)),
                  const((1, KV_LORA_RANK)), const((KV_LORA_RANK, kw)), const((vw, KV_LORA_RANK)),
                  const((1, HEAD_PAD)), const((1, HEAD_PAD)),
                  const((1, gw)), const((GM_GROUPS, CHUNK, CHUNK)), const((CHUNK, GM_GROUPS)), const((1, gw)),
                  rows(LANES), rows(LANES)],
        out_specs=(rows(qw), rows(qw), pl.BlockSpec((1, vw, tm), lambda i: (i, 0, 0)), rows(gw)),
        compiler_params=pltpu.CompilerParams(
            dimension_semantics=("parallel",), vmem_limit_bytes=_vmem_limit(est)),
        name="mix_in",
    )(h, mixg, win, qag, wuq, kvag, wuk, wvt, qg, kg, gvg, ws, bs, gog, cos, sin)


def _attn_kernel(q_ref, k_ref, vt_ref, o_ref, st_ref, *acc_refs, tq, hg):
    i = pl.program_id(2)
    for acc_ref in acc_refs:
        acc_ref[...] = jnp.zeros(acc_ref.shape, F32)

    ones = jnp.ones((SUM_ROWS, tq), BF16)

    def scores(j, slot):
        start = pl.multiple_of(j * tq, tq)
        for hd in range(hg):
            q = q_ref[0, :, hd * HEAD_PAD:(hd + 1) * HEAD_PAD]
            kj = k_ref[0, pl.ds(start, tq), hd * HEAD_PAD:(hd + 1) * HEAD_PAD]
            st_ref[slot, hd] = lax.dot_general(kj, q, (((1,), (1,)), ((), ())), preferred_element_type=F32)

    def softmax(ms, slot, diagonal):
        if diagonal:
            key = lax.broadcasted_iota(jnp.int32, (tq, tq), 0)
            qry = lax.broadcasted_iota(jnp.int32, (tq, tq), 1)
            keep = key <= qry
        new_ms, pts, alphas = [], [], []
        for hd in range(hg):
            st = jnp.where(keep, st_ref[slot, hd], NEG_BIG) if diagonal else st_ref[slot, hd]
            m_new = jnp.maximum(ms[hd], jnp.max(st, axis=0, keepdims=True))
            pts.append(jnp.exp2(st - m_new).astype(BF16))
            alphas.append(jnp.exp2(ms[hd] - m_new))
            new_ms.append(m_new)
        return tuple(new_ms), pts, alphas

    def accumulate(j, pts, alphas):
        for hd in range(hg):
            vt1 = jnp.concatenate([vt_ref[j, hd * V_HEAD_DIM:(hd + 1) * V_HEAD_DIM, :], ones], axis=0)
            acc_refs[hd][...] = alphas[hd] * acc_refs[hd][...] + jnp.dot(
                vt1, pts[hd], preferred_element_type=F32)

    def step(j, ms, diagonal):
        scores(j, 0)
        ms, pts, alphas = softmax(ms, 0, diagonal)
        accumulate(j, pts, alphas)
        return ms

    def pair(jj, ms):
        scores(2 * jj, 0)
        scores(2 * jj + 1, 1)
        ms, pts, alphas = softmax(ms, 0, False)
        accumulate(2 * jj, pts, alphas)
        ms, pts, alphas = softmax(ms, 1, False)
        accumulate(2 * jj + 1, pts, alphas)
        return ms

    init = tuple(jnp.full((1, tq), NEG_BIG, F32) for _ in range(hg))
    ms = lax.fori_loop(0, lax.shift_right_logical(i, 1), pair, init)
    ms = lax.cond(i % 2 == 1, lambda m: step(i - 1, m, False), lambda m: m, ms)
    step(i, ms, True)
    for hd in range(hg):
        acc = acc_refs[hd][...]
        o_ref[0, :, hd * V_HEAD_DIM:(hd + 1) * V_HEAD_DIM] = (
            acc[:V_HEAD_DIM] / acc[V_HEAD_DIM:V_HEAD_DIM + 1]).T


def _attention(q, k, vt, *, tq, hg):
    b, s, _ = q.shape
    nkv = s // tq
    est = 2 * hg * tq * HEAD_PAD * 2 + 2 * hg * s * (HEAD_PAD + V_HEAD_DIM) * 2 + 2 * hg * tq * V_HEAD_DIM * 4 \
        + hg * tq * (V_HEAD_DIM + SUM_ROWS) * 4 + 2 * hg * tq * tq * 4 + 2 * hg * tq * tq * 4
    return pl.pallas_call(
        functools.partial(_attn_kernel, tq=tq, hg=hg),
        out_shape=jax.ShapeDtypeStruct((b, s, MLA_HEADS * V_HEAD_DIM), F32),
        grid=(b, MLA_HEADS // hg, nkv),
        in_specs=[
            pl.BlockSpec((1, tq, hg * HEAD_PAD), lambda bi, hi, qi: (bi, qi, hi)),
            pl.BlockSpec((1, s, hg * HEAD_PAD), lambda bi, hi, qi: (bi, 0, hi)),
            pl.BlockSpec((nkv, hg * V_HEAD_DIM, tq), lambda bi, hi, qi: (bi, hi, 0)),
        ],
        out_specs=pl.BlockSpec((1, tq, hg * V_HEAD_DIM), lambda bi, hi, qi: (bi, qi, hi)),
        scratch_shapes=[pltpu.VMEM((2, hg, tq, tq), F32)]
        + [pltpu.VMEM((V_HEAD_DIM + SUM_ROWS, tq), F32) for _ in range(hg)],
        compiler_params=pltpu.CompilerParams(
            dimension_semantics=("parallel", "parallel", "arbitrary"), vmem_limit_bytes=_vmem_limit(est)),
        name="attn",
    )(q, k, vt)


def _mix_out_kernel(h_ref, a_ref, gn_ref, ag_ref, wout_ref, o_ref):
    an = _rms(a_ref[...], ag_ref[...]).astype(BF16)
    mixed = jnp.concatenate([an, gn_ref[...]], axis=1)
    o_ref[...] = h_ref[...] + jnp.dot(mixed, wout_ref[...], preferred_element_type=F32)


def _mix_out(h, a, gn, ag, wout, layer, *, tm):
    t, d = h.shape
    aw, gw = a.shape[1], gn.shape[1]
    est = (aw + gw) * d * 2 + 2 * tm * (2 * d * 4 + aw * 4 + gw * 2) + 3 * tm * d * 4
    return pl.pallas_call(
        _mix_out_kernel,
        out_shape=jax.ShapeDtypeStruct((t, d), F32),
        grid=(t // tm,),
        in_specs=[
            pl.BlockSpec((tm, d), lambda i: (i, 0)),
            pl.BlockSpec((tm, aw), lambda i: (i, 0)),
            pl.BlockSpec((tm, gw), lambda i: (i, 0)),
            pl.BlockSpec((1, aw), lambda i: (0, 0), pipeline_mode=pl.Buffered(1)),
            pl.BlockSpec((None, aw + gw, d), lambda i: (layer, 0, 0), pipeline_mode=pl.Buffered(1)),
        ],
        out_specs=pl.BlockSpec((tm, d), lambda i: (i, 0)),
        compiler_params=pltpu.CompilerParams(
            dimension_semantics=("parallel",), vmem_limit_bytes=_vmem_limit(est)),
        name="mix_out",
    )(h, a, gn, ag, wout)


def _ple_kernel(h_ref, p_ref, gg_ref, wg_ref, wp_ref, pg_ref, o_ref):
    x = h_ref[...]
    e = _rms(jnp.dot(p_ref[...].astype(BF16), wp_ref[...], preferred_element_type=F32), pg_ref[...])
    gate = jax.nn.sigmoid(
        jnp.dot(_rms(x, gg_ref[...]).astype(BF16), wg_ref[...], preferred_element_type=F32))
    o_ref[...] = x + gate * e


def _ple(h, p, layer, gg, wg, wp, pg, *, tm):
    t, d = h.shape
    pd = p.shape[-1]
    est = (d + pd) * d * 2 + 2 * tm * (2 * d * 4 + pd * 4) + 4 * tm * d * 4
    return pl.pallas_call(
        _ple_kernel,
        out_shape=jax.ShapeDtypeStruct((t, d), F32),
        grid=(t // tm,),
        in_specs=[
            pl.BlockSpec((tm, d), lambda i: (i, 0)),
            pl.BlockSpec((None, tm, pd), lambda i: (layer, i, 0)),
            pl.BlockSpec((1, d), lambda i: (0, 0), pipeline_mode=pl.Buffered(1)),
            pl.BlockSpec((None, d, d), lambda i: (layer, 0, 0), pipeline_mode=pl.Buffered(1)),
            pl.BlockSpec((None, pd, d), lambda i: (layer, 0, 0), pipeline_mode=pl.Buffered(1)),
            pl.BlockSpec((1, d), lambda i: (0, 0), pipeline_mode=pl.Buffered(1)),
        ],
        out_specs=pl.BlockSpec((tm, d), lambda i: (i, 0)),
        compiler_params=pltpu.CompilerParams(
            dimension_semantics=("parallel",), vmem_limit_bytes=_vmem_limit(est)),
        name="ple",
    )(h, p, gg, wg, wp, pg)


def _cast_pad_kernel(w_ref, o_ref, *, axis, size):
    x = w_ref[...].astype(BF16)
    if axis == 0:
        o_ref[:size, :] = x
        o_ref[size:, :] = jnp.zeros((o_ref.shape[0] - size, o_ref.shape[1]), BF16)
    else:
        o_ref[:, :size] = x
        o_ref[:, size:] = jnp.zeros((o_ref.shape[0], o_ref.shape[1] - size), BF16)


def _cast_pad(w, *, axis, padded, blk):
    nl, r, c = w.shape
    if axis == 0:
        in_blk, out_blk, out_shape, grid = (None, r, blk), (None, padded, blk), (nl, padded, c), (nl, c // blk)
        idx = lambda l, j: (l, 0, j)
    else:
        in_blk, out_blk, out_shape, grid = (None, blk, c), (None, blk, padded), (nl, r, padded), (nl, r // blk)
        idx = lambda l, j: (l, j, 0)
    in_bytes, out_bytes = (r, c)[axis] * blk * 4, padded * blk * 2
    est = 3 * in_bytes + 2 * out_bytes
    return pl.pallas_call(
        functools.partial(_cast_pad_kernel, axis=axis, size=(r, c)[axis]),
        out_shape=jax.ShapeDtypeStruct(out_shape, BF16),
        grid=grid,
        in_specs=[pl.BlockSpec(in_blk, idx)],
        out_specs=pl.BlockSpec(out_blk, idx),
        compiler_params=pltpu.CompilerParams(
            dimension_semantics=("parallel", "parallel"), vmem_limit_bytes=_vmem_limit(est)),
        name="cast_pad",
    )(w)


def _ffn_weights(w1, w3, w2, *, tf):
    f = w1.shape[2]
    fp = -(-f // tf) * tf
    return (_cast_pad(w1, axis=1, padded=fp, blk=256), _cast_pad(w3, axis=1, padded=fp, blk=256),
            _cast_pad(w2, axis=0, padded=fp, blk=256))


def _dup_rope(x, axis_len_nope):
    return jnp.concatenate([x, x[..., axis_len_nope:]], axis=-1)


def kernel(x, p, positions, ffn_a_norm, ffn_a_w1, ffn_a_w3, ffn_a_w2, mix_norm, w_in, q_a_norm, w_uq, kv_a_norm, w_ukv, q_norm, k_norm, gm_v_norm, gm_ws, gm_bs, attn_out_norm, gm_out_norm, w_out, ffn_b_norm, ffn_b_w1, ffn_b_w3, ffn_b_w2, ple_gate_norm, w_ple_gate, w_ple, ple_norm):
    b, s, d = x.shape
    depth = p.shape[0]
    t = b * s
    tm_ffn, tf = 1024, 512
    ffn_a = _ffn_weights(ffn_a_w1, ffn_a_w3, ffn_a_w2, tf=tf)
    ffn_b = _ffn_weights(ffn_b_w1, ffn_b_w3, ffn_b_w2, tf=tf)
    wout_b, wgate_b, wple_b = w_out.astype(BF16), w_ple_gate.astype(BF16), w_ple.astype(BF16)
    tm_out, tm_ple, tq, heads_per_step = 512, 512, 512, 4

    cos, sin = _rope_tables(positions.astype(F32).reshape(t, 1), tm=1024)
    p2 = p.reshape(depth, t, p.shape[-1])
    h = x.reshape(t, d)
    o_kr = Q_LORA_RANK + KV_LORA_RANK
    for i in range(depth):
        h = _ffn(h, ffn_a_norm[i][None, :], *ffn_a, i, tm=tm_ffn, tf=tf)

        win = w_in[i].astype(BF16)
        win = jnp.concatenate(
            [win[:, :o_kr + QK_ROPE_DIM], win[:, o_kr:o_kr + QK_ROPE_DIM], win[:, o_kr + QK_ROPE_DIM:]], axis=1)
        wuq = _dup_rope(w_uq[i].astype(BF16).reshape(Q_LORA_RANK, MLA_HEADS, QK_HEAD_DIM), QK_NOPE_DIM)
        wuq = wuq.reshape(Q_LORA_RANK, MLA_HEADS * HEAD_PAD)
        wukv = w_ukv[i].astype(BF16).reshape(KV_LORA_RANK, MLA_HEADS, QK_NOPE_DIM + V_HEAD_DIM)
        wuk = wukv[:, :, :QK_NOPE_DIM].reshape(KV_LORA_RANK, MLA_HEADS * QK_NOPE_DIM)
        wvt = wukv[:, :, QK_NOPE_DIM:].reshape(KV_LORA_RANK, MLA_HEADS * V_HEAD_DIM).T
        q, k, vt, gn = _mix_in(
            h, mix_norm[i][None, :], win, q_a_norm[i][None, :], wuq, kv_a_norm[i][None, :],
            wuk, wvt, _dup_rope(q_norm[i], QK_NOPE_DIM)[None, :],
            _dup_rope(k_norm[i], QK_NOPE_DIM)[None, :], gm_v_norm[i][None, :], gm_ws[i].astype(BF16),
            gm_bs[i].T, gm_out_norm[i][None, :], cos, sin, tm=tq)
        a = _attention(q.reshape(b, s, -1), k.reshape(b, s, -1), vt, tq=tq, hg=heads_per_step)
        h = _mix_out(h, a.reshape(t, -1), gn, attn_out_norm[i][None, :], wout_b, i, tm=tm_out)

        h = _ffn(h, ffn_b_norm[i][None, :], *ffn_b, i, tm=tm_ffn, tf=tf)

        h = _ple(h, p2, i, ple_gate_norm[i][None, :], wgate_b, wple_b, ple_norm[i][None, :], tm=tm_ple)
    return h.reshape(b, s, d)
```

```python
import functools
import math

import jax
import jax.numpy as jnp
from jax import lax
from jax.experimental import pallas as pl
from jax.experimental.pallas import tpu as pltpu

F32 = jnp.float32
BF16 = jnp.bfloat16

MLA_HEADS = 8
QK_NOPE_DIM = 128
QK_ROPE_DIM = 64
QK_HEAD_DIM = QK_NOPE_DIM + QK_ROPE_DIM
V_HEAD_DIM = 128
Q_LORA_RANK = 512
KV_LORA_RANK = 256
GM_GROUPS = 8
GM_GROUP_DIM = 128
CHUNK = 128
ROPE_BASE = 10000.0
EPS = 1e-6

LANES = 128
V7X_VMEM_LIMIT_CAP = 60000 * 1024

HEAD_PAD = 2 * LANES
NEG_BIG = -1e30
SUM_ROWS = 16
LOG2E = math.log2(math.e)


def _vmem_limit(nbytes):
    assert nbytes <= V7X_VMEM_LIMIT_CAP, nbytes
    return int(nbytes)


def _rms(x, g):
    ms = jnp.mean(x * x, axis=-1, keepdims=True)
    return x * lax.rsqrt(ms + EPS) * g


def _rope_kernel(pos_ref, invf_ref, cmask_ref, smask_ref, cos_ref, sin_ref):
    ang = pos_ref[...] * invf_ref[...]
    cos_ref[...] = jnp.cos(ang) * cmask_ref[...]
    sin_ref[...] = jnp.sin(ang) * smask_ref[...]


def _rope_tables(pos_f, *, tm):
    t = pos_f.shape[0]
    half = QK_ROPE_DIM // 2
    inv_freq = ROPE_BASE ** (-jnp.arange(0, QK_ROPE_DIM, 2, dtype=F32) / QK_ROPE_DIM)
    invf = jnp.tile(inv_freq, LANES // half)[None, :]
    ones, zeros = jnp.ones((half,), F32), jnp.zeros((half,), F32)
    cmask = jnp.concatenate([ones, ones, zeros, zeros])[None, :]
    smask = jnp.concatenate([-ones, ones, zeros, zeros])[None, :]
    row = pl.BlockSpec((1, LANES), lambda i: (0, 0))
    tab = pl.BlockSpec((tm, LANES), lambda i: (i, 0))
    return pl.pallas_call(
        _rope_kernel,
        out_shape=(jax.ShapeDtypeStruct((t, LANES), F32), jax.ShapeDtypeStruct((t, LANES), F32)),
        grid=(t // tm,),
        in_specs=[pl.BlockSpec((tm, 1), lambda i: (i, 0)), row, row, row],
        out_specs=(tab, tab),
        compiler_params=pltpu.CompilerParams(dimension_semantics=("parallel",)),
        name="rope_tables",
    )(pos_f, invf, cmask, smask)


def _ffn_kernel(h_ref, g_ref, w1_ref, w3_ref, w2_ref, o_ref, n_ref):
    k = pl.program_id(1)

    def half_swiglu(n):
        a = jnp.dot(n, w1_ref[...], preferred_element_type=F32)
        b = jnp.dot(n, w3_ref[...], preferred_element_type=F32)
        mid = (0.5 * a) * jax.nn.sigmoid(a) * b
        return jnp.dot(mid.astype(BF16), w2_ref[...], preferred_element_type=F32)

    @pl.when(k == 0)
    def _():
        x = h_ref[...]
        n = _rms(x, g_ref[...]).astype(BF16)
        n_ref[...] = n
        o_ref[...] = x + half_swiglu(n)

    @pl.when(k > 0)
    def _():
        o_ref[...] += half_swiglu(n_ref[...])


def _ffn(h, g, w1, w3, w2, layer, *, tm, tf):
    t, d = h.shape
    f = w2.shape[1]
    est = 2 * tm * d * 4 + 2 * tm * d * 4 + tm * d * 2 + 2 * 3 * d * tf * 2 + 4 * tm * tf * 4
    return pl.pallas_call(
        _ffn_kernel,
        out_shape=jax.ShapeDtypeStruct((t, d), F32),
        grid=(t // tm, f // tf),
        in_specs=[
            pl.BlockSpec((tm, d), lambda i, k: (i, 0)),
            pl.BlockSpec((1, d), lambda i, k: (0, 0)),
            pl.BlockSpec((None, d, tf), lambda i, k: (layer, 0, k)),
            pl.BlockSpec((None, d, tf), lambda i, k: (layer, 0, k)),
            pl.BlockSpec((None, tf, d), lambda i, k: (layer, k, 0)),
        ],
        out_specs=pl.BlockSpec((tm, d), lambda i, k: (i, 0)),
        scratch_shapes=[pltpu.VMEM((tm, d), BF16)],
        compiler_params=pltpu.CompilerParams(
            dimension_semantics=("parallel", "arbitrary"), vmem_limit_bytes=_vmem_limit(est)),
        name="ffn",
    )(h, g, w1, w3, w2)


def _mix_in_kernel(h_ref, mixg_ref, win_ref, qag_ref, wuq_ref, kvag_ref, wuk_ref, wvt_ref, qg_ref, kg_ref,
                   gvg_ref, ws_ref, bs_ref, gog_ref, cos_ref, sin_ref,
                   q_ref, k_ref, vt_ref, gn_ref, *, tm):
    n = _rms(h_ref[...], mixg_ref[...]).astype(BF16)
    o_kv = Q_LORA_RANK
    o_kr = o_kv + KV_LORA_RANK
    o_u = o_kr + LANES
    o_v = o_u + GM_GROUPS * GM_GROUP_DIM
    zc = jnp.dot(n, win_ref[:, :o_u], preferred_element_type=F32)
    c_q, c_kv, kr = zc[:, :o_kv], zc[:, o_kv:o_kr], zc[:, o_kr:o_u]
    qf = jnp.dot(_rms(c_q, qag_ref[...]).astype(BF16), wuq_ref[...], preferred_element_type=F32)
    kvn = _rms(c_kv, kvag_ref[...]).astype(BF16)
    kf = jnp.dot(kvn, wuk_ref[...], preferred_element_type=F32)
    v = jnp.dot(n, win_ref[:, o_v:], preferred_element_type=F32)
    u = jnp.dot(n, win_ref[:, o_u:o_v], preferred_element_type=F32)
    cos, sin = cos_ref[...], sin_ref[...]
    inv_hd = 1.0 / QK_HEAD_DIM

    def rope(x):
        return x * cos + pltpu.roll(x, QK_ROPE_DIM // 2, 1) * sin

    qg = qg_ref[...]
    q_scale = QK_HEAD_DIM ** -0.5 * LOG2E
    for hd in range(MLA_HEADS):
        nope = qf[:, hd * HEAD_PAD: hd * HEAD_PAD + LANES]
        rt = qf[:, hd * HEAD_PAD + LANES: (hd + 1) * HEAD_PAD]
        ss = jnp.sum(nope * nope + 0.5 * (rt * rt), axis=-1, keepdims=True)
        r = lax.rsqrt(ss * inv_hd + EPS) * q_scale
        q_ref[:, hd * HEAD_PAD: hd * HEAD_PAD + LANES] = (nope * r * qg[:, :LANES]).astype(BF16)
        q_ref[:, hd * HEAD_PAD + LANES: (hd + 1) * HEAD_PAD] = rope(rt * r * qg[:, LANES:]).astype(BF16)

    kg = kg_ref[...]
    kr_rot = rope(kr * kg[:, LANES:])
    ss_r = 0.5 * jnp.sum(kr * kr, axis=-1, keepdims=True)
    for hd in range(MLA_HEADS):
        kn = kf[:, hd * QK_NOPE_DIM: (hd + 1) * QK_NOPE_DIM]
        ss = jnp.sum(kn * kn, axis=-1, keepdims=True) + ss_r
        r = lax.rsqrt(ss * inv_hd + EPS)
        k_ref[:, hd * HEAD_PAD: hd * HEAD_PAD + LANES] = (kn * r * kg[:, :LANES]).astype(BF16)
        k_ref[:, hd * HEAD_PAD + LANES: (hd + 1) * HEAD_PAD] = (kr_rot * r).astype(BF16)

    ug = jax.nn.gelu(u)
    vn = _rms(jax.nn.gelu(v), gvg_ref[...]).astype(BF16)
    nchunk = tm // CHUNK
    trow = lax.broadcasted_iota(jnp.int32, (CHUNK, CHUNK), 0)
    scol = lax.broadcasted_iota(jnp.int32, (CHUNK, CHUNK), 1)
    cols = []
    for g in range(GM_GROUPS):
        wc = jnp.where(scol <= trow, ws_ref[g], jnp.zeros((), BF16))
        lo, hi = g * GM_GROUP_DIM, (g + 1) * GM_GROUP_DIM
        rhs = jnp.concatenate([vn[c * CHUNK:(c + 1) * CHUNK, lo:hi] for c in range(nchunk)], axis=1)
        gate = jnp.dot(wc, rhs, preferred_element_type=F32) + bs_ref[:, g:g + 1]
        cols.append(jnp.concatenate(
            [gate[:, c * GM_GROUP_DIM:(c + 1) * GM_GROUP_DIM] for c in range(nchunk)], axis=0))
    vt_ref[0] = lax.dot_general(wvt_ref[...], kvn, (((1,), (1,)), ((), ())),
                                preferred_element_type=F32).astype(BF16)
    g_out = ug * jnp.concatenate(cols, axis=1)
    gn_ref[...] = _rms(g_out, gog_ref[...]).astype(BF16)


def _mix_in(h, mixg, win, qag, wuq, kvag, wuk, wvt, qg, kg, gvg, ws, bs, gog, cos, sin, *, tm):
    t, d = h.shape
    zc = win.shape[1]
    qw, kw, vw, gw = MLA_HEADS * HEAD_PAD, MLA_HEADS * QK_NOPE_DIM, MLA_HEADS * V_HEAD_DIM, GM_GROUPS * GM_GROUP_DIM

    def const(shape):
        return pl.BlockSpec(shape, lambda i: (0,) * len(shape), pipeline_mode=pl.Buffered(1))

    def rows(width):
        return pl.BlockSpec((tm, width), lambda i: (i, 0))

    weights = (d * zc + Q_LORA_RANK * qw + KV_LORA_RANK * (kw + vw) + GM_GROUPS * CHUNK * CHUNK) * 2
    est = weights + 2 * tm * d * 4 + 2 * tm * (2 * qw + vw + gw) * 2 + 4 * tm * LANES * 4 + 3 * tm * zc * 4
    return pl.pallas_call(
        functools.partial(_mix_in_kernel, tm=tm),
        out_shape=(jax.ShapeDtypeStruct((t, qw), BF16), jax.ShapeDtypeStruct((t, qw), BF16),
                   jax.ShapeDtypeStruct((t // tm, vw, tm), BF16), jax.ShapeDtypeStruct((t, gw), BF16)),
        grid=(t // tm,),
        in_specs=[rows(d), const((1, d)), const((d, zc)), const((1, Q_LORA_RANK)), const((Q_LORA_RANK, qw)),
                  const((1, KV_LORA_RANK)), const((KV_LORA_RANK, kw)), const((vw, KV_LORA_RANK)),
                  const((1, HEAD_PAD)), const((1, HEAD_PAD)),
                  const((1, gw)), const((GM_GROUPS, CHUNK, CHUNK)), const((CHUNK, GM_GROUPS)), const((1, gw)),
                  rows(LANES), rows(LANES)],
        out_specs=(rows(qw), rows(qw), pl.BlockSpec((1, vw, tm), lambda i: (i, 0, 0)), rows(gw)),
        compiler_params=pltpu.CompilerParams(
            dimension_semantics=("parallel",), vmem_limit_bytes=_vmem_limit(est)),
        name="mix_in",
    )(h, mixg, win, qag, wuq, kvag, wuk, wvt, qg, kg, gvg, ws, bs, gog, cos, sin)


def _attn_kernel(q_ref, k_ref, vt_ref, o_ref, st_ref, *acc_refs, tq, hg):
    i = pl.program_id(2)
    for acc_ref in acc_refs:
        acc_ref[...] = jnp.zeros(acc_ref.shape, F32)

    ones = jnp.ones((SUM_ROWS, tq), BF16)

    def scores(j, slot):
        start = pl.multiple_of(j * tq, tq)
        for hd in range(hg):
            q = q_ref[0, :, hd * HEAD_PAD:(hd + 1) * HEAD_PAD]
            kj = k_ref[0, pl.ds(start, tq), hd * HEAD_PAD:(hd + 1) * HEAD_PAD]
            st_ref[slot, hd] = lax.dot_general(kj, q, (((1,), (1,)), ((), ())), preferred_element_type=F32)

    def softmax(ms, slot, diagonal):
        if diagonal:
            key = lax.broadcasted_iota(jnp.int32, (tq, tq), 0)
            qry = lax.broadcasted_iota(jnp.int32, (tq, tq), 1)
            keep = key <= qry
        new_ms, pts, alphas = [], [], []
        for hd in range(hg):
            st = jnp.where(keep, st_ref[slot, hd], NEG_BIG) if diagonal else st_ref[slot, hd]
            m_new = jnp.maximum(ms[hd], jnp.max(st, axis=0, keepdims=True))
            pts.append(jnp.exp2(st - m_new).astype(BF16))
            alphas.append(jnp.exp2(ms[hd] - m_new))
            new_ms.append(m_new)
        return tuple(new_ms), pts, alphas

    def accumulate(j, pts, alphas):
        for hd in range(hg):
            vt1 = jnp.concatenate([vt_ref[j, hd * V_HEAD_DIM:(hd + 1) * V_HEAD_DIM, :], ones], axis=0)
            acc_refs[hd][...] = alphas[hd] * acc_refs[hd][...] + jnp.dot(
                vt1, pts[hd], preferred_element_type=F32)

    def step(j, ms, diagonal):
        scores(j, 0)
        ms, pts, alphas = softmax(ms, 0, diagonal)
        accumulate(j, pts, alphas)
        return ms

    def pair(jj, ms):
        scores(2 * jj, 0)
        scores(2 * jj + 1, 1)
        ms, pts, alphas = softmax(ms, 0, False)
        accumulate(2 * jj, pts, alphas)
        ms, pts, alphas = softmax(ms, 1, False)
        accumulate(2 * jj + 1, pts, alphas)
        return ms

    init = tuple(jnp.full((1, tq), NEG_BIG, F32) for _ in range(hg))
    ms = lax.fori_loop(0, lax.shift_right_logical(i, 1), pair, init)
    ms = lax.cond(i % 2 == 1, lambda m: step(i - 1, m, False), lambda m: m, ms)
    step(i, ms, True)
    for hd in range(hg):
        acc = acc_refs[hd][...]
        o_ref[0, :, hd * V_HEAD_DIM:(hd + 1) * V_HEAD_DIM] = (
            acc[:V_HEAD_DIM] / acc[V_HEAD_DIM:V_HEAD_DIM + 1]).T


def _attention(q, k, vt, *, tq, hg):
    b, s, _ = q.shape
    nkv = s // tq
    est = 2 * hg * tq * HEAD_PAD * 2 + 2 * hg * s * (HEAD_PAD + V_HEAD_DIM) * 2 + 2 * hg * tq * V_HEAD_DIM * 4 \
        + hg * tq * (V_HEAD_DIM + SUM_ROWS) * 4 + 2 * hg * tq * tq * 4 + 2 * hg * tq * tq * 4
    return pl.pallas_call(
        functools.partial(_attn_kernel, tq=tq, hg=hg),
        out_shape=jax.ShapeDtypeStruct((b, s, MLA_HEADS * V_HEAD_DIM), F32),
        grid=(b, MLA_HEADS // hg, nkv),
        in_specs=[
            pl.BlockSpec((1, tq, hg * HEAD_PAD), lambda bi, hi, qi: (bi, qi, hi)),
            pl.BlockSpec((1, s, hg * HEAD_PAD), lambda bi, hi, qi: (bi, 0, hi)),
            pl.BlockSpec((nkv, hg * V_HEAD_DIM, tq), lambda bi, hi, qi: (bi, hi, 0)),
        ],
        out_specs=pl.BlockSpec((1, tq, hg * V_HEAD_DIM), lambda bi, hi, qi: (bi, qi, hi)),
        scratch_shapes=[pltpu.VMEM((2, hg, tq, tq), F32)]
        + [pltpu.VMEM((V_HEAD_DIM + SUM_ROWS, tq), F32) for _ in range(hg)],
        compiler_params=pltpu.CompilerParams(
            dimension_semantics=("parallel", "parallel", "arbitrary"), vmem_limit_bytes=_vmem_limit(est)),
        name="attn",
    )(q, k, vt)


def _mix_out_kernel(h_ref, a_ref, gn_ref, ag_ref, wout_ref, o_ref):
    an = _rms(a_ref[...], ag_ref[...]).astype(BF16)
    mixed = jnp.concatenate([an, gn_ref[...]], axis=1)
    o_ref[...] = h_ref[...] + jnp.dot(mixed, wout_ref[...], preferred_element_type=F32)


def _mix_out(h, a, gn, ag, wout, layer, *, tm):
    t, d = h.shape
    aw, gw = a.shape[1], gn.shape[1]
    est = (aw + gw) * d * 2 + 2 * tm * (2 * d * 4 + aw * 4 + gw * 2) + 3 * tm * d * 4
    return pl.pallas_call(
        _mix_out_kernel,
        out_shape=jax.ShapeDtypeStruct((t, d), F32),
        grid=(t // tm,),
        in_specs=[
            pl.BlockSpec((tm, d), lambda i: (i, 0)),
            pl.BlockSpec((tm, aw), lambda i: (i, 0)),
            pl.BlockSpec((tm, gw), lambda i: (i, 0)),
            pl.BlockSpec((1, aw), lambda i: (0, 0), pipeline_mode=pl.Buffered(1)),
            pl.BlockSpec((None, aw + gw, d), lambda i: (layer, 0, 0), pipeline_mode=pl.Buffered(1)),
        ],
        out_specs=pl.BlockSpec((tm, d), lambda i: (i, 0)),
        compiler_params=pltpu.CompilerParams(
            dimension_semantics=("parallel",), vmem_limit_bytes=_vmem_limit(est)),
        name="mix_out",
    )(h, a, gn, ag, wout)


def _ple_kernel(h_ref, p_ref, gg_ref, wg_ref, wp_ref, pg_ref, o_ref):
    x = h_ref[...]
    e = _rms(jnp.dot(p_ref[...].astype(BF16), wp_ref[...], preferred_element_type=F32), pg_ref[...])
    gate = jax.nn.sigmoid(
        jnp.dot(_rms(x, gg_ref[...]).astype(BF16), wg_ref[...], preferred_element_type=F32))
    o_ref[...] = x + gate * e


def _ple(h, p, layer, gg, wg, wp, pg, *, tm):
    t, d = h.shape
    pd = p.shape[-1]
    est = (d + pd) * d * 2 + 2 * tm * (2 * d * 4 + pd * 4) + 4 * tm * d * 4
    return pl.pallas_call(
        _ple_kernel,
        out_shape=jax.ShapeDtypeStruct((t, d), F32),
        grid=(t // tm,),
        in_specs=[
            pl.BlockSpec((tm, d), lambda i: (i, 0)),
            pl.BlockSpec((None, tm, pd), lambda i: (layer, i, 0)),
            pl.BlockSpec((1, d), lambda i: (0, 0), pipeline_mode=pl.Buffered(1)),
            pl.BlockSpec((None, d, d), lambda i: (layer, 0, 0), pipeline_mode=pl.Buffered(1)),
            pl.BlockSpec((None, pd, d), lambda i: (layer, 0, 0), pipeline_mode=pl.Buffered(1)),
            pl.BlockSpec((1, d), lambda i: (0, 0), pipeline_mode=pl.Buffered(1)),
        ],
        out_specs=pl.BlockSpec((tm, d), lambda i: (i, 0)),
        compiler_params=pltpu.CompilerParams(
            dimension_semantics=("parallel",), vmem_limit_bytes=_vmem_limit(est)),
        name="ple",
    )(h, p, gg, wg, wp, pg)


def _cast_pad_kernel(w_ref, o_ref, *, axis, size):
    x = w_ref[...].astype(BF16)
    if axis == 0:
        o_ref[:size, :] = x
        o_ref[size:, :] = jnp.zeros((o_ref.shape[0] - size, o_ref.shape[1]), BF16)
    else:
        o_ref[:, :size] = x
        o_ref[:, size:] = jnp.zeros((o_ref.shape[0], o_ref.shape[1] - size), BF16)


def _cast_pad(w, *, axis, padded, blk):
    nl, r, c = w.shape
    if axis == 0:
        in_blk, out_blk, out_shape, grid = (None, r, blk), (None, padded, blk), (nl, padded, c), (nl, c // blk)
        idx = lambda l, j: (l, 0, j)
    else:
        in_blk, out_blk, out_shape, grid = (None, blk, c), (None, blk, padded), (nl, r, padded), (nl, r // blk)
        idx = lambda l, j: (l, j, 0)
    in_bytes, out_bytes = (r, c)[axis] * blk * 4, padded * blk * 2
    est = 3 * in_bytes + 2 * out_bytes
    return pl.pallas_call(
        functools.partial(_cast_pad_kernel, axis=axis, size=(r, c)[axis]),
        out_shape=jax.ShapeDtypeStruct(out_shape, BF16),
        grid=grid,
        in_specs=[pl.BlockSpec(in_blk, idx)],
        out_specs=pl.BlockSpec(out_blk, idx),
        compiler_params=pltpu.CompilerParams(
            dimension_semantics=("parallel", "parallel"), vmem_limit_bytes=_vmem_limit(est)),
        name="cast_pad",
    )(w)


def _ffn_weights(w1, w3, w2, *, tf):
    f = w1.shape[2]
    fp = -(-f // tf) * tf
    return (_cast_pad(w1, axis=1, padded=fp, blk=256), _cast_pad(w3, axis=1, padded=fp, blk=256),
            _cast_pad(w2, axis=0, padded=fp, blk=256))


def _dup_rope(x, axis_len_nope):
    return jnp.concatenate([x, x[..., axis_len_nope:]], axis=-1)


def kernel(x, p, positions, ffn_a_norm, ffn_a_w1, ffn_a_w3, ffn_a_w2, mix_norm, w_in, q_a_norm, w_uq, kv_a_norm, w_ukv, q_norm, k_norm, gm_v_norm, gm_ws, gm_bs, attn_out_norm, gm_out_norm, w_out, ffn_b_norm, ffn_b_w1, ffn_b_w3, ffn_b_w2, ple_gate_norm, w_ple_gate, w_ple, ple_norm):
    b, s, d = x.shape
    depth = p.shape[0]
    t = b * s
    tm_ffn, tf = 1024, 512
    ffn_a = _ffn_weights(ffn_a_w1, ffn_a_w3, ffn_a_w2, tf=tf)
    ffn_b = _ffn_weights(ffn_b_w1, ffn_b_w3, ffn_b_w2, tf=tf)
    wout_b, wgate_b, wple_b = w_out.astype(BF16), w_ple_gate.astype(BF16), w_ple.astype(BF16)
    tm_out, tm_ple, tq, heads_per_step = 512, 512, 512, 4

    cos, sin = _rope_tables(positions.astype(F32).reshape(t, 1), tm=1024)
    p2 = p.reshape(depth, t, p.shape[-1])
    h = x.reshape(t, d)
    o_kr = Q_LORA_RANK + KV_LORA_RANK
    for i in range(depth):
        h = _ffn(h, ffn_a_norm[i][None, :], *ffn_a, i, tm=tm_ffn, tf=tf)

        win = w_in[i].astype(BF16)
        win = jnp.concatenate(
            [win[:, :o_kr + QK_ROPE_DIM], win[:, o_kr:o_kr + QK_ROPE_DIM], win[:, o_kr + QK_ROPE_DIM:]], axis=1)
        wuq = _dup_rope(w_uq[i].astype(BF16).reshape(Q_LORA_RANK, MLA_HEADS, QK_HEAD_DIM), QK_NOPE_DIM)
        wuq = wuq.reshape(Q_LORA_RANK, MLA_HEADS * HEAD_PAD)
        wukv = w_ukv[i].astype(BF16).reshape(KV_LORA_RANK, MLA_HEADS, QK_NOPE_DIM + V_HEAD_DIM)
        wuk = wukv[:, :, :QK_NOPE_DIM].reshape(KV_LORA_RANK, MLA_HEADS * QK_NOPE_DIM)
        wvt = wukv[:, :, QK_NOPE_DIM:].reshape(KV_LORA_RANK, MLA_HEADS * V_HEAD_DIM).T
        q, k, vt, gn = _mix_in(
            h, mix_norm[i][None, :], win, q_a_norm[i][None, :], wuq, kv_a_norm[i][None, :],
            wuk, wvt, _dup_rope(q_norm[i], QK_NOPE_DIM)[None, :],
            _dup_rope(k_norm[i], QK_NOPE_DIM)[None, :], gm_v_norm[i][None, :], gm_ws[i].astype(BF16),
            gm_bs[i].T, gm_out_norm[i][None, :], cos, sin, tm=tq)
        a = _attention(q.reshape(b, s, -1), k.reshape(b, s, -1), vt, tq=tq, hg=heads_per_step)
        h = _mix_out(h, a.reshape(t, -1), gn, attn_out_norm[i][None, :], wout_b, i, tm=tm_out)

        h = _ffn(h, ffn_b_norm[i][None, :], *ffn_b, i, tm=tm_ffn, tf=tf)

        h = _ple(h, p2, i, ple_gate_norm[i][None, :], wgate_b, wple_b, ple_norm[i][None, :], tm=tm_ple)
    return h.reshape(b, s, d)
```

```python
import functools
import math

import jax
import jax.numpy as jnp
from jax import lax
from jax.experimental import pallas as pl
from jax.experimental.pallas import tpu as pltpu

F32 = jnp.float32
BF16 = jnp.bfloat16

MLA_HEADS = 8
QK_NOPE_DIM = 128
QK_ROPE_DIM = 64
QK_HEAD_DIM = QK_NOPE_DIM + QK_ROPE_DIM
V_HEAD_DIM = 128
Q_LORA_RANK = 512
KV_LORA_RANK = 256
GM_GROUPS = 8
GM_GROUP_DIM = 128
CHUNK = 128
ROPE_BASE = 10000.0
EPS = 1e-6

LANES = 128
V7X_VMEM_LIMIT_CAP = 60000 * 1024

HEAD_PAD = 2 * LANES
NEG_BIG = -1e30
SUM_ROWS = 16
LOG2E = math.log2(math.e)


def _vmem_limit(nbytes):
    assert nbytes <= V7X_VMEM_LIMIT_CAP, nbytes
    return int(nbytes)


def _rms(x, g):
    ms = jnp.mean(x * x, axis=-1, keepdims=True)
    return x * lax.rsqrt(ms + EPS) * g


def _rope_kernel(pos_ref, invf_ref, cmask_ref, smask_ref, cos_ref, sin_ref):
    ang = pos_ref[...] * invf_ref[...]
    cos_ref[...] = jnp.cos(ang) * cmask_ref[...]
    sin_ref[...] = jnp.sin(ang) * smask_ref[...]


def _rope_tables(pos_f, *, tm):
    t = pos_f.shape[0]
    half = QK_ROPE_DIM // 2
    inv_freq = ROPE_BASE ** (-jnp.arange(0, QK_ROPE_DIM, 2, dtype=F32) / QK_ROPE_DIM)
    invf = jnp.tile(inv_freq, LANES // half)[None, :]
    ones, zeros = jnp.ones((half,), F32), jnp.zeros((half,), F32)
    cmask = jnp.concatenate([ones, ones, zeros, zeros])[None, :]
    smask = jnp.concatenate([-ones, ones, zeros, zeros])[None, :]
    row = pl.BlockSpec((1, LANES), lambda i: (0, 0))
    tab = pl.BlockSpec((tm, LANES), lambda i: (i, 0))
    return pl.pallas_call(
        _rope_kernel,
        out_shape=(jax.ShapeDtypeStruct((t, LANES), F32), jax.ShapeDtypeStruct((t, LANES), F32)),
        grid=(t // tm,),
        in_specs=[pl.BlockSpec((tm, 1), lambda i: (i, 0)), row, row, row],
        out_specs=(tab, tab),
        compiler_params=pltpu.CompilerParams(dimension_semantics=("parallel",)),
        name="rope_tables",
    )(pos_f, invf, cmask, smask)


def _ffn_kernel(h_ref, g_ref, w1_ref, w3_ref, w2_ref, o_ref, n_ref):
    k = pl.program_id(1)

    def half_swiglu(n):
        a = jnp.dot(n, w1_ref[...], preferred_element_type=F32)
        b = jnp.dot(n, w3_ref[...], preferred_element_type=F32)
        mid = (0.5 * a) * jax.nn.sigmoid(a) * b
        return jnp.dot(mid.astype(BF16), w2_ref[...], preferred_element_type=F32)

    @pl.when(k == 0)
    def _():
        x = h_ref[...]
        n = _rms(x, g_ref[...]).astype(BF16)
        n_ref[...] = n
        o_ref[...] = x + half_swiglu(n)

    @pl.when(k > 0)
    def _():
        o_ref[...] += half_swiglu(n_ref[...])


def _ffn(h, g, w1, w3, w2, layer, *, tm, tf):
    t, d = h.shape
    f = w2.shape[1]
    est = 2 * tm * d * 4 + 2 * tm * d * 4 + tm * d * 2 + 2 * 3 * d * tf * 2 + 4 * tm * tf * 4
    return pl.pallas_call(
        _ffn_kernel,
        out_shape=jax.ShapeDtypeStruct((t, d), F32),
        grid=(t // tm, f // tf),
        in_specs=[
            pl.BlockSpec((tm, d), lambda i, k: (i, 0)),
            pl.BlockSpec((1, d), lambda i, k: (0, 0)),
            pl.BlockSpec((None, d, tf), lambda i, k: (layer, 0, k)),
            pl.BlockSpec((None, d, tf), lambda i, k: (layer, 0, k)),
            pl.BlockSpec((None, tf, d), lambda i, k: (layer, k, 0)),
        ],
        out_specs=pl.BlockSpec((tm, d), lambda i, k: (i, 0)),
        scratch_shapes=[pltpu.VMEM((tm, d), BF16)],
        compiler_params=pltpu.CompilerParams(
            dimension_semantics=("parallel", "arbitrary"), vmem_limit_bytes=_vmem_limit(est)),
        name="ffn",
    )(h, g, w1, w3, w2)


def _mix_in_kernel(h_ref, mixg_ref, win_ref, qag_ref, wuq_ref, kvag_ref, wuk_ref, wvt_ref, qg_ref, kg_ref,
                   gvg_ref, ws_ref, bs_ref, gog_ref, cos_ref, sin_ref,
                   q_ref, k_ref, vt_ref, gn_ref, *, tm):
    n = _rms(h_ref[...], mixg_ref[...]).astype(BF16)
    o_kv = Q_LORA_RANK
    o_kr = o_kv + KV_LORA_RANK
    o_u = o_kr + LANES
    o_v = o_u + GM_GROUPS * GM_GROUP_DIM
    zc = jnp.dot(n, win_ref[:, :o_u], preferred_element_type=F32)
    c_q, c_kv, kr = zc[:, :o_kv], zc[:, o_kv:o_kr], zc[:, o_kr:o_u]
    qf = jnp.dot(_rms(c_q, qag_ref[...]).astype(BF16), wuq_ref[...], preferred_element_type=F32)
    kvn = _rms(c_kv, kvag_ref[...]).astype(BF16)
    kf = jnp.dot(kvn, wuk_ref[...], preferred_element_type=F32)
    v = jnp.dot(n, win_ref[:, o_v:], preferred_element_type=F32)
    u = jnp.dot(n, win_ref[:, o_u:o_v], preferred_element_type=F32)
    cos, sin = cos_ref[...], sin_ref[...]
    inv_hd = 1.0 / QK_HEAD_DIM

    def rope(x):
        return x * cos + pltpu.roll(x, QK_ROPE_DIM // 2, 1) * sin

    qg = qg_ref[...]
    q_scale = QK_HEAD_DIM ** -0.5 * LOG2E
    for hd in range(MLA_HEADS):
        nope = qf[:, hd * HEAD_PAD: hd * HEAD_PAD + LANES]
        rt = qf[:, hd * HEAD_PAD + LANES: (hd + 1) * HEAD_PAD]
        ss = jnp.sum(nope * nope + 0.5 * (rt * rt), axis=-1, keepdims=True)
        r = lax.rsqrt(ss * inv_hd + EPS) * q_scale
        q_ref[:, hd * HEAD_PAD: hd * HEAD_PAD + LANES] = (nope * r * qg[:, :LANES]).astype(BF16)
        q_ref[:, hd * HEAD_PAD + LANES: (hd + 1) * HEAD_PAD] = rope(rt * r * qg[:, LANES:]).astype(BF16)

    kg = kg_ref[...]
    kr_rot = rope(kr * kg[:, LANES:])
    ss_r = 0.5 * jnp.sum(kr * kr, axis=-1, keepdims=True)
    for hd in range(MLA_HEADS):
        kn = kf[:, hd * QK_NOPE_DIM: (hd + 1) * QK_NOPE_DIM]
        ss = jnp.sum(kn * kn, axis=-1, keepdims=True) + ss_r
        r = lax.rsqrt(ss * inv_hd + EPS)
        k_ref[:, hd * HEAD_PAD: hd * HEAD_PAD + LANES] = (kn * r * kg[:, :LANES]).astype(BF16)
        k_ref[:, hd * HEAD_PAD + LANES: (hd + 1) * HEAD_PAD] = (kr_rot * r).astype(BF16)

    ug = jax.nn.gelu(u)
    vn = _rms(jax.nn.gelu(v), gvg_ref[...]).astype(BF16)
    nchunk = tm // CHUNK
    trow = lax.broadcasted_iota(jnp.int32, (CHUNK, CHUNK), 0)
    scol = lax.broadcasted_iota(jnp.int32, (CHUNK, CHUNK), 1)
    cols = []
    for g in range(GM_GROUPS):
        wc = jnp.where(scol <= trow, ws_ref[g], jnp.zeros((), BF16))
        lo, hi = g * GM_GROUP_DIM, (g + 1) * GM_GROUP_DIM
        rhs = jnp.concatenate([vn[c * CHUNK:(c + 1) * CHUNK, lo:hi] for c in range(nchunk)], axis=1)
        gate = jnp.dot(wc, rhs, preferred_element_type=F32) + bs_ref[:, g:g + 1]
        cols.append(jnp.concatenate(
            [gate[:, c * GM_GROUP_DIM:(c + 1) * GM_GROUP_DIM] for c in range(nchunk)], axis=0))
    vt_ref[0] = lax.dot_general(wvt_ref[...], kvn, (((1,), (1,)), ((), ())),
                                preferred_element_type=F32).astype(BF16)
    g_out = ug * jnp.concatenate(cols, axis=1)
    gn_ref[...] = _rms(g_out, gog_ref[...]).astype(BF16)


def _mix_in(h, mixg, win, qag, wuq, kvag, wuk, wvt, qg, kg, gvg, ws, bs, gog, cos, sin, *, tm):
    t, d = h.shape
    zc = win.shape[1]
    qw, kw, vw, gw = MLA_HEADS * HEAD_PAD, MLA_HEADS * QK_NOPE_DIM, MLA_HEADS * V_HEAD_DIM, GM_GROUPS * GM_GROUP_DIM

    def const(shape):
        return pl.BlockSpec(shape, lambda i: (0,) * len(shape), pipeline_mode=pl.Buffered(1))

    def rows(width):
        return pl.BlockSpec((tm, width), lambda i: (i, 0))

    weights = (d * zc + Q_LORA_RANK * qw + KV_LORA_RANK * (kw + vw) + GM_GROUPS * CHUNK * CHUNK) * 2
    est = weights + 2 * tm * d * 4 + 2 * tm * (2 * qw + vw + gw) * 2 + 4 * tm * LANES * 4 + 3 * tm * zc * 4
    return pl.pallas_call(
        functools.partial(_mix_in_kernel, tm=tm),
        out_shape=(jax.ShapeDtypeStruct((t, qw), BF16), jax.ShapeDtypeStruct((t, qw), BF16),
                   jax.ShapeDtypeStruct((t // tm, vw, tm), BF16), jax.ShapeDtypeStruct((t, gw), BF16)),
        grid=(t // tm,),
        in_specs=[rows(d), const((1, d)), const((d, zc)), const((1, Q_LORA_RANK)), const((Q_LORA_RANK, qw)),
                  const((1, KV_LORA_RANK)), const((KV_LORA_RANK, kw)), const((vw, KV_LORA_RANK)),
                  const((1, HEAD_PAD)), const((1, HEAD_PAD)),
                  const((1, gw)), const((GM_GROUPS, CHUNK, CHUNK)), const((CHUNK, GM_GROUPS)), const((1, gw)),
                  rows(LANES), rows(LANES)],
        out_specs=(rows(qw), rows(qw), pl.BlockSpec((1, vw, tm), lambda i: (i, 0, 0)), rows(gw)),
        compiler_params=pltpu.CompilerParams(
            dimension_semantics=("parallel",), vmem_limit_bytes=_vmem_limit(est)),
        name="mix_in",
    )(h, mixg, win, qag, wuq, kvag, wuk, wvt, qg, kg, gvg, ws, bs, gog, cos, sin)


def _attn_kernel(q_ref, k_ref, vt_ref, o_ref, st_ref, *acc_refs, tq, hg):
    i = pl.program_id(2)
    for acc_ref in acc_refs:
        acc_ref[...] = jnp.zeros(acc_ref.shape, F32)

    ones = jnp.ones((SUM_ROWS, tq), BF16)

    def scores(j, slot):
        start = pl.multiple_of(j * tq, tq)
        for hd in range(hg):
            q = q_ref[0, :, hd * HEAD_PAD:(hd + 1) * HEAD_PAD]
            kj = k_ref[0, pl.ds(start, tq), hd * HEAD_PAD:(hd + 1) * HEAD_PAD]
            st_ref[slot, hd] = lax.dot_general(kj, q, (((1,), (1,)), ((), ())), preferred_element_type=F32)

    def softmax(ms, slot, diagonal):
        if diagonal:
            key = lax.broadcasted_iota(jnp.int32, (tq, tq), 0)
            qry = lax.broadcasted_iota(jnp.int32, (tq, tq), 1)
            keep = key <= qry
        new_ms, pts, alphas = [], [], []
        for hd in range(hg):
            st = jnp.where(keep, st_ref[slot, hd], NEG_BIG) if diagonal else st_ref[slot, hd]
            m_new = jnp.maximum(ms[hd], jnp.max(st, axis=0, keepdims=True))
            pts.append(jnp.exp2(st - m_new).astype(BF16))
            alphas.append(jnp.exp2(ms[hd] - m_new))
            new_ms.append(m_new)
        return tuple(new_ms), pts, alphas

    def accumulate(j, pts, alphas):
        for hd in range(hg):
            vt1 = jnp.concatenate([vt_ref[j, hd * V_HEAD_DIM:(hd + 1) * V_HEAD_DIM, :], ones], axis=0)
            acc_refs[hd][...] = alphas[hd] * acc_refs[hd][...] + jnp.dot(
                vt1, pts[hd], preferred_element_type=F32)

    def step(j, ms, diagonal):
        scores(j, 0)
        ms, pts, alphas = softmax(ms, 0, diagonal)
        accumulate(j, pts, alphas)
        return ms

    def pair(jj, ms):
        scores(2 * jj, 0)
        scores(2 * jj + 1, 1)
        ms, pts, alphas = softmax(ms, 0, False)
        accumulate(2 * jj, pts, alphas)
        ms, pts, alphas = softmax(ms, 1, False)
        accumulate(2 * jj + 1, pts, alphas)
        return ms

    init = tuple(jnp.full((1, tq), NEG_BIG, F32) for _ in range(hg))
    ms = lax.fori_loop(0, lax.shift_right_logical(i, 1), pair, init)
    ms = lax.cond(i % 2 == 1, lambda m: step(i - 1, m, False), lambda m: m, ms)
    step(i, ms, True)
    for hd in range(hg):
        acc = acc_refs[hd][...]
        o_ref[0, :, hd * V_HEAD_DIM:(hd + 1) * V_HEAD_DIM] = (
            acc[:V_HEAD_DIM] / acc[V_HEAD_DIM:V_HEAD_DIM + 1]).T


def _attention(q, k, vt, *, tq, hg):
    b, s, _ = q.shape
    nkv = s // tq
    est = 2 * hg * tq * HEAD_PAD * 2 + 2 * hg * s * (HEAD_PAD + V_HEAD_DIM) * 2 + 2 * hg * tq * V_HEAD_DIM * 4 \
        + hg * tq * (V_HEAD_DIM + SUM_ROWS) * 4 + 2 * hg * tq * tq * 4 + 2 * hg * tq * tq * 4
    return pl.pallas_call(
        functools.partial(_attn_kernel, tq=tq, hg=hg),
        out_shape=jax.ShapeDtypeStruct((b, s, MLA_HEADS * V_HEAD_DIM), F32),
        grid=(b, MLA_HEADS // hg, nkv),
        in_specs=[
            pl.BlockSpec((1, tq, hg * HEAD_PAD), lambda bi, hi, qi: (bi, qi, hi)),
            pl.BlockSpec((1, s, hg * HEAD_PAD), lambda bi, hi, qi: (bi, 0, hi)),
            pl.BlockSpec((nkv, hg * V_HEAD_DIM, tq), lambda bi, hi, qi: (bi, hi, 0)),
        ],
        out_specs=pl.BlockSpec((1, tq, hg * V_HEAD_DIM), lambda bi, hi, qi: (bi, qi, hi)),
        scratch_shapes=[pltpu.VMEM((2, hg, tq, tq), F32)]
        + [pltpu.VMEM((V_HEAD_DIM + SUM_ROWS, tq), F32) for _ in range(hg)],
        compiler_params=pltpu.CompilerParams(
            dimension_semantics=("parallel", "parallel", "arbitrary"), vmem_limit_bytes=_vmem_limit(est)),
        name="attn",
    )(q, k, vt)


def _mix_out_kernel(h_ref, a_ref, gn_ref, ag_ref, wout_ref, w1_ref, w3_ref, w2_ref,
                    o_ref, c1_ref, c3_ref, c2_ref, *, nvalid):
    an = _rms(a_ref[...], ag_ref[...]).astype(BF16)
    mixed = jnp.concatenate([an, gn_ref[...]], axis=1)
    o_ref[...] = h_ref[...] + jnp.dot(mixed, wout_ref[...], preferred_element_type=F32)
    f = w1_ref.shape[1]
    for w_ref, c_ref in ((w1_ref, c1_ref), (w3_ref, c3_ref)):
        c_ref[:, :f] = w_ref[...].astype(BF16)
        c_ref[:, f:] = jnp.zeros((c_ref.shape[0], c_ref.shape[1] - f), BF16)
    piece = w2_ref[...].astype(BF16)
    c2_ref[...] = jnp.where(pl.program_id(0) < nvalid, piece, jnp.zeros_like(piece))


def _mix_out(h, a, gn, ag, wout, w1, w3, w2, layer, *, tm, fp):
    t, d = h.shape
    aw, gw = a.shape[1], gn.shape[1]
    f = w1.shape[2]
    nsteps = t // tm
    rows, piece = d // nsteps, fp - f
    nvalid = f // piece
    assert d % nsteps == 0 and rows % SUM_ROWS == 0 and f % piece == 0 and fp // piece <= nsteps
    ride = 2 * (2 * rows * f * 4 + piece * d * 4) + 2 * (2 * rows * fp * 2 + piece * d * 2)
    est = (aw + gw) * d * 2 + 2 * tm * (2 * d * 4 + aw * 4 + gw * 2) + 3 * tm * d * 4 + ride
    slab_in = pl.BlockSpec((None, rows, f), lambda i: (layer, i, 0))
    slab_out = pl.BlockSpec((rows, fp), lambda i: (i, 0))
    return pl.pallas_call(
        functools.partial(_mix_out_kernel, nvalid=nvalid),
        out_shape=(jax.ShapeDtypeStruct((t, d), F32), jax.ShapeDtypeStruct((d, fp), BF16),
                   jax.ShapeDtypeStruct((d, fp), BF16), jax.ShapeDtypeStruct((fp, d), BF16)),
        grid=(nsteps,),
        in_specs=[
            pl.BlockSpec((tm, d), lambda i: (i, 0)),
            pl.BlockSpec((tm, aw), lambda i: (i, 0)),
            pl.BlockSpec((tm, gw), lambda i: (i, 0)),
            pl.BlockSpec((1, aw), lambda i: (0, 0), pipeline_mode=pl.Buffered(1)),
            pl.BlockSpec((None, aw + gw, d), lambda i: (layer, 0, 0), pipeline_mode=pl.Buffered(1)),
            slab_in, slab_in,
            pl.BlockSpec((None, piece, d), lambda i: (layer, jnp.minimum(i, nvalid - 1), 0)),
        ],
        out_specs=(pl.BlockSpec((tm, d), lambda i: (i, 0)), slab_out, slab_out,
                   pl.BlockSpec((piece, d), lambda i: (jnp.minimum(i, nvalid), 0))),
        compiler_params=pltpu.CompilerParams(
            dimension_semantics=("arbitrary",), vmem_limit_bytes=_vmem_limit(est)),
        name="mix_out",
    )(h, a, gn, ag, wout, w1, w3, w2)


def _ple_kernel(h_ref, p_ref, gg_ref, wg_ref, wp_ref, pg_ref, o_ref):
    x = h_ref[...]
    e = _rms(jnp.dot(p_ref[...].astype(BF16), wp_ref[...], preferred_element_type=F32), pg_ref[...])
    gate = jax.nn.sigmoid(
        jnp.dot(_rms(x, gg_ref[...]).astype(BF16), wg_ref[...], preferred_element_type=F32))
    o_ref[...] = x + gate * e


def _ple(h, p, layer, gg, wg, wp, pg, *, tm):
    t, d = h.shape
    pd = p.shape[-1]
    est = (d + pd) * d * 2 + 2 * tm * (2 * d * 4 + pd * 4) + 4 * tm * d * 4
    return pl.pallas_call(
        _ple_kernel,
        out_shape=jax.ShapeDtypeStruct((t, d), F32),
        grid=(t // tm,),
        in_specs=[
            pl.BlockSpec((tm, d), lambda i: (i, 0)),
            pl.BlockSpec((None, tm, pd), lambda i: (layer, i, 0)),
            pl.BlockSpec((1, d), lambda i: (0, 0), pipeline_mode=pl.Buffered(1)),
            pl.BlockSpec((None, d, d), lambda i: (layer, 0, 0), pipeline_mode=pl.Buffered(1)),
            pl.BlockSpec((None, pd, d), lambda i: (layer, 0, 0), pipeline_mode=pl.Buffered(1)),
            pl.BlockSpec((1, d), lambda i: (0, 0), pipeline_mode=pl.Buffered(1)),
        ],
        out_specs=pl.BlockSpec((tm, d), lambda i: (i, 0)),
        compiler_params=pltpu.CompilerParams(
            dimension_semantics=("parallel",), vmem_limit_bytes=_vmem_limit(est)),
        name="ple",
    )(h, p, gg, wg, wp, pg)


def _cast_pad_kernel(w_ref, o_ref, *, axis, size):
    x = w_ref[...].astype(BF16)
    if axis == 0:
        o_ref[:size, :] = x
        o_ref[size:, :] = jnp.zeros((o_ref.shape[0] - size, o_ref.shape[1]), BF16)
    else:
        o_ref[:, :size] = x
        o_ref[:, size:] = jnp.zeros((o_ref.shape[0], o_ref.shape[1] - size), BF16)


def _cast_pad(w, *, axis, padded, blk):
    nl, r, c = w.shape
    if axis == 0:
        in_blk, out_blk, out_shape, grid = (None, r, blk), (None, padded, blk), (nl, padded, c), (nl, c // blk)
        idx = lambda l, j: (l, 0, j)
    else:
        in_blk, out_blk, out_shape, grid = (None, blk, c), (None, blk, padded), (nl, r, padded), (nl, r // blk)
        idx = lambda l, j: (l, j, 0)
    in_bytes, out_bytes = (r, c)[axis] * blk * 4, padded * blk * 2
    est = 3 * in_bytes + 2 * out_bytes
    return pl.pallas_call(
        functools.partial(_cast_pad_kernel, axis=axis, size=(r, c)[axis]),
        out_shape=jax.ShapeDtypeStruct(out_shape, BF16),
        grid=grid,
        in_specs=[pl.BlockSpec(in_blk, idx)],
        out_specs=pl.BlockSpec(out_blk, idx),
        compiler_params=pltpu.CompilerParams(
            dimension_semantics=("parallel", "parallel"), vmem_limit_bytes=_vmem_limit(est)),
        name="cast_pad",
    )(w)


def _ffn_weights(w1, w3, w2, *, tf):
    f = w1.shape[2]
    fp = -(-f // tf) * tf
    return (_cast_pad(w1, axis=1, padded=fp, blk=256), _cast_pad(w3, axis=1, padded=fp, blk=256),
            _cast_pad(w2, axis=0, padded=fp, blk=256))


def _dup_rope(x, axis_len_nope):
    return jnp.concatenate([x, x[..., axis_len_nope:]], axis=-1)


def kernel(x, p, positions, ffn_a_norm, ffn_a_w1, ffn_a_w3, ffn_a_w2, mix_norm, w_in, q_a_norm, w_uq, kv_a_norm, w_ukv, q_norm, k_norm, gm_v_norm, gm_ws, gm_bs, attn_out_norm, gm_out_norm, w_out, ffn_b_norm, ffn_b_w1, ffn_b_w3, ffn_b_w2, ple_gate_norm, w_ple_gate, w_ple, ple_norm):
    b, s, d = x.shape
    depth = p.shape[0]
    t = b * s
    tm_ffn, tf = 1024, 512
    ffn_a = _ffn_weights(ffn_a_w1, ffn_a_w3, ffn_a_w2, tf=tf)
    wout_b, wgate_b, wple_b = w_out.astype(BF16), w_ple_gate.astype(BF16), w_ple.astype(BF16)
    tm_out, tm_ple, tq, heads_per_step = 512, 512, 512, 4

    cos, sin = _rope_tables(positions.astype(F32).reshape(t, 1), tm=1024)
    p2 = p.reshape(depth, t, p.shape[-1])
    h = x.reshape(t, d)
    o_kr = Q_LORA_RANK + KV_LORA_RANK
    for i in range(depth):
        h = _ffn(h, ffn_a_norm[i][None, :], *ffn_a, i, tm=tm_ffn, tf=tf)

        win = w_in[i].astype(BF16)
        win = jnp.concatenate(
            [win[:, :o_kr + QK_ROPE_DIM], win[:, o_kr:o_kr + QK_ROPE_DIM], win[:, o_kr + QK_ROPE_DIM:]], axis=1)
        wuq = _dup_rope(w_uq[i].astype(BF16).reshape(Q_LORA_RANK, MLA_HEADS, QK_HEAD_DIM), QK_NOPE_DIM)
        wuq = wuq.reshape(Q_LORA_RANK, MLA_HEADS * HEAD_PAD)
        wukv = w_ukv[i].astype(BF16).reshape(KV_LORA_RANK, MLA_HEADS, QK_NOPE_DIM + V_HEAD_DIM)
        wuk = wukv[:, :, :QK_NOPE_DIM].reshape(KV_LORA_RANK, MLA_HEADS * QK_NOPE_DIM)
        wvt = wukv[:, :, QK_NOPE_DIM:].reshape(KV_LORA_RANK, MLA_HEADS * V_HEAD_DIM).T
        q, k, vt, gn = _mix_in(
            h, mix_norm[i][None, :], win, q_a_norm[i][None, :], wuq, kv_a_norm[i][None, :],
            wuk, wvt, _dup_rope(q_norm[i], QK_NOPE_DIM)[None, :],
            _dup_rope(k_norm[i], QK_NOPE_DIM)[None, :], gm_v_norm[i][None, :], gm_ws[i].astype(BF16),
            gm_bs[i].T, gm_out_norm[i][None, :], cos, sin, tm=tq)
        a = _attention(q.reshape(b, s, -1), k.reshape(b, s, -1), vt, tq=tq, hg=heads_per_step)
        h, *ffn_b = _mix_out(h, a.reshape(t, -1), gn, attn_out_norm[i][None, :], wout_b,
                             ffn_b_w1, ffn_b_w3, ffn_b_w2, i, tm=tm_out, fp=ffn_a[0].shape[2])

        h = _ffn(h, ffn_b_norm[i][None, :], *(w[None] for w in ffn_b), 0, tm=tm_ffn, tf=tf)

        h = _ple(h, p2, i, ple_gate_norm[i][None, :], wgate_b, wple_b, ple_norm[i][None, :], tm=tm_ple)
    return h.reshape(b, s, d)
```

```python
import functools
import math

import jax
import jax.numpy as jnp
from jax import lax
from jax.experimental import pallas as pl
from jax.experimental.pallas import tpu as pltpu

F32 = jnp.float32
BF16 = jnp.bfloat16

MLA_HEADS = 8
QK_NOPE_DIM = 128
QK_ROPE_DIM = 64
QK_HEAD_DIM = QK_NOPE_DIM + QK_ROPE_DIM
V_HEAD_DIM = 128
Q_LORA_RANK = 512
KV_LORA_RANK = 256
GM_GROUPS = 8
GM_GROUP_DIM = 128
CHUNK = 128
ROPE_BASE = 10000.0
EPS = 1e-6

LANES = 128
V7X_VMEM_LIMIT_CAP = 60000 * 1024

HEAD_PAD = 2 * LANES
NEG_BIG = -1e30
SUM_ROWS = 16
LOG2E = math.log2(math.e)


def _vmem_limit(nbytes):
    assert nbytes <= V7X_VMEM_LIMIT_CAP, nbytes
    return int(nbytes)


def _rms(x, g):
    ms = jnp.mean(x * x, axis=-1, keepdims=True)
    return x * lax.rsqrt(ms + EPS) * g


def _rope_kernel(pos_ref, invf_ref, cmask_ref, smask_ref, cos_ref, sin_ref):
    ang = pos_ref[...] * invf_ref[...]
    cos_ref[...] = jnp.cos(ang) * cmask_ref[...]
    sin_ref[...] = jnp.sin(ang) * smask_ref[...]


def _rope_tables(pos_f, *, tm):
    t = pos_f.shape[0]
    half = QK_ROPE_DIM // 2
    inv_freq = ROPE_BASE ** (-jnp.arange(0, QK_ROPE_DIM, 2, dtype=F32) / QK_ROPE_DIM)
    invf = jnp.tile(inv_freq, LANES // half)[None, :]
    ones, zeros = jnp.ones((half,), F32), jnp.zeros((half,), F32)
    cmask = jnp.concatenate([ones, ones, zeros, zeros])[None, :]
    smask = jnp.concatenate([-ones, ones, zeros, zeros])[None, :]
    row = pl.BlockSpec((1, LANES), lambda i: (0, 0))
    tab = pl.BlockSpec((tm, LANES), lambda i: (i, 0))
    return pl.pallas_call(
        _rope_kernel,
        out_shape=(jax.ShapeDtypeStruct((t, LANES), F32), jax.ShapeDtypeStruct((t, LANES), F32)),
        grid=(t // tm,),
        in_specs=[pl.BlockSpec((tm, 1), lambda i: (i, 0)), row, row, row],
        out_specs=(tab, tab),
        compiler_params=pltpu.CompilerParams(dimension_semantics=("parallel",)),
        name="rope_tables",
    )(pos_f, invf, cmask, smask)


def _ffn_kernel(h_ref, g_ref, w1_ref, w3_ref, w2_ref, o_ref, n_ref):
    k = pl.program_id(1)

    def half_swiglu(n):
        a = jnp.dot(n, w1_ref[...], preferred_element_type=F32)
        b = jnp.dot(n, w3_ref[...], preferred_element_type=F32)
        mid = (0.5 * a) * jax.nn.sigmoid(a) * b
        return jnp.dot(mid.astype(BF16), w2_ref[...], preferred_element_type=F32)

    @pl.when(k == 0)
    def _():
        x = h_ref[...]
        n = _rms(x, g_ref[...]).astype(BF16)
        n_ref[...] = n
        o_ref[...] = x + half_swiglu(n)

    @pl.when(k > 0)
    def _():
        o_ref[...] += half_swiglu(n_ref[...])


def _ffn(h, g, w1, w3, w2, layer, *, tm, tf):
    t, d = h.shape
    f = w2.shape[1]
    est = 2 * tm * d * 4 + 2 * tm * d * 4 + tm * d * 2 + 2 * 3 * d * tf * 2 + 4 * tm * tf * 4
    return pl.pallas_call(
        _ffn_kernel,
        out_shape=jax.ShapeDtypeStruct((t, d), F32),
        grid=(t // tm, f // tf),
        in_specs=[
            pl.BlockSpec((tm, d), lambda i, k: (i, 0)),
            pl.BlockSpec((1, d), lambda i, k: (0, 0)),
            pl.BlockSpec((None, d, tf), lambda i, k: (layer, 0, k)),
            pl.BlockSpec((None, d, tf), lambda i, k: (layer, 0, k)),
            pl.BlockSpec((None, tf, d), lambda i, k: (layer, k, 0)),
        ],
        out_specs=pl.BlockSpec((tm, d), lambda i, k: (i, 0)),
        scratch_shapes=[pltpu.VMEM((tm, d), BF16)],
        compiler_params=pltpu.CompilerParams(
            dimension_semantics=("parallel", "arbitrary"), vmem_limit_bytes=_vmem_limit(est)),
        name="ffn",
    )(h, g, w1, w3, w2)


def _mix_in_kernel(h_ref, mixg_ref, win_ref, qag_ref, wuq_ref, kvag_ref, wuk_ref, wvt_ref, qg_ref, kg_ref,
                   gvg_ref, ws_ref, bs_ref, gog_ref, cos_ref, sin_ref,
                   q_ref, k_ref, vt_ref, gn_ref, *, tm):
    n = _rms(h_ref[...], mixg_ref[...]).astype(BF16)
    o_kv = Q_LORA_RANK
    o_kr = o_kv + KV_LORA_RANK
    o_u = o_kr + LANES
    o_v = o_u + GM_GROUPS * GM_GROUP_DIM
    zc = jnp.dot(n, win_ref[:, :o_u], preferred_element_type=F32)
    c_q, c_kv, kr = zc[:, :o_kv], zc[:, o_kv:o_kr], zc[:, o_kr:o_u]
    qf = jnp.dot(_rms(c_q, qag_ref[...]).astype(BF16), wuq_ref[...], preferred_element_type=F32)
    kvn = _rms(c_kv, kvag_ref[...]).astype(BF16)
    kf = jnp.dot(kvn, wuk_ref[...], preferred_element_type=F32)
    v = jnp.dot(n, win_ref[:, o_v:], preferred_element_type=F32)
    u = jnp.dot(n, win_ref[:, o_u:o_v], preferred_element_type=F32)
    cos, sin = cos_ref[...], sin_ref[...]
    inv_hd = 1.0 / QK_HEAD_DIM

    def rope(x):
        return x * cos + pltpu.roll(x, QK_ROPE_DIM // 2, 1) * sin

    qg = qg_ref[...]
    q_scale = QK_HEAD_DIM ** -0.5 * LOG2E
    for hd in range(MLA_HEADS):
        nope = qf[:, hd * HEAD_PAD: hd * HEAD_PAD + LANES]
        rt = qf[:, hd * HEAD_PAD + LANES: (hd + 1) * HEAD_PAD]
        ss = jnp.sum(nope * nope + 0.5 * (rt * rt), axis=-1, keepdims=True)
        r = lax.rsqrt(ss * inv_hd + EPS) * q_scale
        q_ref[:, hd * HEAD_PAD: hd * HEAD_PAD + LANES] = (nope * r * qg[:, :LANES]).astype(BF16)
        q_ref[:, hd * HEAD_PAD + LANES: (hd + 1) * HEAD_PAD] = rope(rt * r * qg[:, LANES:]).astype(BF16)

    kg = kg_ref[...]
    kr_rot = rope(kr * kg[:, LANES:])
    ss_r = 0.5 * jnp.sum(kr * kr, axis=-1, keepdims=True)
    for hd in range(MLA_HEADS):
        kn = kf[:, hd * QK_NOPE_DIM: (hd + 1) * QK_NOPE_DIM]
        ss = jnp.sum(kn * kn, axis=-1, keepdims=True) + ss_r
        r = lax.rsqrt(ss * inv_hd + EPS)
        k_ref[:, hd * HEAD_PAD: hd * HEAD_PAD + LANES] = (kn * r * kg[:, :LANES]).astype(BF16)
        k_ref[:, hd * HEAD_PAD + LANES: (hd + 1) * HEAD_PAD] = (kr_rot * r).astype(BF16)

    ug = jax.nn.gelu(u)
    vn = _rms(jax.nn.gelu(v), gvg_ref[...]).astype(BF16)
    nchunk = tm // CHUNK
    trow = lax.broadcasted_iota(jnp.int32, (CHUNK, CHUNK), 0)
    scol = lax.broadcasted_iota(jnp.int32, (CHUNK, CHUNK), 1)
    cols = []
    for g in range(GM_GROUPS):
        wc = jnp.where(scol <= trow, ws_ref[g], jnp.zeros((), BF16))
        lo, hi = g * GM_GROUP_DIM, (g + 1) * GM_GROUP_DIM
        rhs = jnp.concatenate([vn[c * CHUNK:(c + 1) * CHUNK, lo:hi] for c in range(nchunk)], axis=1)
        gate = jnp.dot(wc, rhs, preferred_element_type=F32) + bs_ref[:, g:g + 1]
        cols.append(jnp.concatenate(
            [gate[:, c * GM_GROUP_DIM:(c + 1) * GM_GROUP_DIM] for c in range(nchunk)], axis=0))
    vt_ref[0] = lax.dot_general(wvt_ref[...], kvn, (((1,), (1,)), ((), ())),
                                preferred_element_type=F32).astype(BF16)
    g_out = ug * jnp.concatenate(cols, axis=1)
    gn_ref[...] = _rms(g_out, gog_ref[...]).astype(BF16)


def _mix_in(h, mixg, win, qag, wuq, kvag, wuk, wvt, qg, kg, gvg, ws, bs, gog, cos, sin, *, tm):
    t, d = h.shape
    zc = win.shape[1]
    qw, kw, vw, gw = MLA_HEADS * HEAD_PAD, MLA_HEADS * QK_NOPE_DIM, MLA_HEADS * V_HEAD_DIM, GM_GROUPS * GM_GROUP_DIM

    def const(shape):
        return pl.BlockSpec(shape, lambda i: (0,) * len(shape), pipeline_mode=pl.Buffered(1))

    def rows(width):
        return pl.BlockSpec((tm, width), lambda i: (i, 0))

    weights = (d * zc + Q_LORA_RANK * qw + KV_LORA_RANK * (kw + vw) + GM_GROUPS * CHUNK * CHUNK) * 2
    est = weights + 2 * tm * d * 4 + 2 * tm * (2 * qw + vw + gw) * 2 + 4 * tm * LANES * 4 + 3 * tm * zc * 4
    return pl.pallas_call(
        functools.partial(_mix_in_kernel, tm=tm),
        out_shape=(jax.ShapeDtypeStruct((t, qw), BF16), jax.ShapeDtypeStruct((t, qw), BF16),
                   jax.ShapeDtypeStruct((t // tm, vw, tm), BF16), jax.ShapeDtypeStruct((t, gw), BF16)),
        grid=(t // tm,),
        in_specs=[rows(d), const((1, d)), const((d, zc)), const((1, Q_LORA_RANK)), const((Q_LORA_RANK, qw)),
                  const((1, KV_LORA_RANK)), const((KV_LORA_RANK, kw)), const((vw, KV_LORA_RANK)),
                  const((1, HEAD_PAD)), const((1, HEAD_PAD)),
                  const((1, gw)), const((GM_GROUPS, CHUNK, CHUNK)), const((CHUNK, GM_GROUPS)), const((1, gw)),
                  rows(LANES), rows(LANES)],
        out_specs=(rows(qw), rows(qw), pl.BlockSpec((1, vw, tm), lambda i: (i, 0, 0)), rows(gw)),
        compiler_params=pltpu.CompilerParams(
            dimension_semantics=("parallel",), vmem_limit_bytes=_vmem_limit(est)),
        name="mix_in",
    )(h, mixg, win, qag, wuq, kvag, wuk, wvt, qg, kg, gvg, ws, bs, gog, cos, sin)


def _attn_kernel(q_ref, k_ref, vt_ref, o_ref, st_ref, *acc_refs, tq, hg):
    i = pl.program_id(2)
    for acc_ref in acc_refs:
        acc_ref[...] = jnp.zeros(acc_ref.shape, F32)

    ones = jnp.ones((SUM_ROWS, tq), BF16)

    def scores(j, slot):
        start = pl.multiple_of(j * tq, tq)
        for hd in range(hg):
            q = q_ref[0, :, hd * HEAD_PAD:(hd + 1) * HEAD_PAD]
            kj = k_ref[0, pl.ds(start, tq), hd * HEAD_PAD:(hd + 1) * HEAD_PAD]
            st_ref[slot, hd] = lax.dot_general(kj, q, (((1,), (1,)), ((), ())), preferred_element_type=F32)

    def softmax(ms, slot, diagonal):
        if diagonal:
            key = lax.broadcasted_iota(jnp.int32, (tq, tq), 0)
            qry = lax.broadcasted_iota(jnp.int32, (tq, tq), 1)
            keep = key <= qry
        new_ms, pts, alphas = [], [], []
        for hd in range(hg):
            st = jnp.where(keep, st_ref[slot, hd], NEG_BIG) if diagonal else st_ref[slot, hd]
            m_new = jnp.maximum(ms[hd], jnp.max(st, axis=0, keepdims=True))
            pts.append(jnp.exp2(st - m_new).astype(BF16))
            alphas.append(jnp.exp2(ms[hd] - m_new))
            new_ms.append(m_new)
        return tuple(new_ms), pts, alphas

    def accumulate(j, pts, alphas):
        for hd in range(hg):
            vt1 = jnp.concatenate([vt_ref[j, hd * V_HEAD_DIM:(hd + 1) * V_HEAD_DIM, :], ones], axis=0)
            acc_refs[hd][...] = alphas[hd] * acc_refs[hd][...] + jnp.dot(
                vt1, pts[hd], preferred_element_type=F32)

    def step(j, ms, diagonal):
        scores(j, 0)
        ms, pts, alphas = softmax(ms, 0, diagonal)
        accumulate(j, pts, alphas)
        return ms

    def pair(jj, ms):
        scores(2 * jj, 0)
        scores(2 * jj + 1, 1)
        ms, pts, alphas = softmax(ms, 0, False)
        accumulate(2 * jj, pts, alphas)
        ms, pts, alphas = softmax(ms, 1, False)
        accumulate(2 * jj + 1, pts, alphas)
        return ms

    init = tuple(jnp.full((1, tq), NEG_BIG, F32) for _ in range(hg))
    ms = lax.fori_loop(0, lax.shift_right_logical(i, 1), pair, init)
    ms = lax.cond(i % 2 == 1, lambda m: step(i - 1, m, False), lambda m: m, ms)
    step(i, ms, True)
    for hd in range(hg):
        acc = acc_refs[hd][...]
        o_ref[0, :, hd * V_HEAD_DIM:(hd + 1) * V_HEAD_DIM] = (
            acc[:V_HEAD_DIM] / acc[V_HEAD_DIM:V_HEAD_DIM + 1]).T


def _attention(q, k, vt, *, tq, hg):
    b, s, _ = q.shape
    nkv = s // tq
    est = 2 * hg * tq * HEAD_PAD * 2 + 2 * hg * s * (HEAD_PAD + V_HEAD_DIM) * 2 + 2 * hg * tq * V_HEAD_DIM * 4 \
        + hg * tq * (V_HEAD_DIM + SUM_ROWS) * 4 + 2 * hg * tq * tq * 4 + 2 * hg * tq * tq * 4
    return pl.pallas_call(
        functools.partial(_attn_kernel, tq=tq, hg=hg),
        out_shape=jax.ShapeDtypeStruct((b, s, MLA_HEADS * V_HEAD_DIM), F32),
        grid=(b, MLA_HEADS // hg, nkv),
        in_specs=[
            pl.BlockSpec((1, tq, hg * HEAD_PAD), lambda bi, hi, qi: (bi, qi, hi)),
            pl.BlockSpec((1, s, hg * HEAD_PAD), lambda bi, hi, qi: (bi, 0, hi)),
            pl.BlockSpec((nkv, hg * V_HEAD_DIM, tq), lambda bi, hi, qi: (bi, hi, 0)),
        ],
        out_specs=pl.BlockSpec((1, tq, hg * V_HEAD_DIM), lambda bi, hi, qi: (bi, qi, hi)),
        scratch_shapes=[pltpu.VMEM((2, hg, tq, tq), F32)]
        + [pltpu.VMEM((V_HEAD_DIM + SUM_ROWS, tq), F32) for _ in range(hg)],
        compiler_params=pltpu.CompilerParams(
            dimension_semantics=("parallel", "parallel", "arbitrary"), vmem_limit_bytes=_vmem_limit(est)),
        name="attn",
    )(q, k, vt)


def _ride_copy(w1_ref, w3_ref, w2_ref, c1_ref, c3_ref, c2_ref, nvalid):
    f = w1_ref.shape[1]
    for w_ref, c_ref in ((w1_ref, c1_ref), (w3_ref, c3_ref)):
        c_ref[:, :f] = w_ref[...].astype(BF16)
        c_ref[:, f:] = jnp.zeros((c_ref.shape[0], c_ref.shape[1] - f), BF16)
    piece = w2_ref[...].astype(BF16)
    c2_ref[...] = jnp.where(pl.program_id(0) < nvalid, piece, jnp.zeros_like(piece))


def _ride_specs(w1, layer, nsteps, fp):
    _, d, f = w1.shape
    rows, piece = d // nsteps, fp - f
    nvalid = f // piece
    assert d % nsteps == 0 and rows % SUM_ROWS == 0 and f % piece == 0 and fp // piece <= nsteps
    slab_in = pl.BlockSpec((None, rows, f), lambda i: (layer, i, 0))
    slab_out = pl.BlockSpec((rows, fp), lambda i: (i, 0))
    in_specs = [slab_in, slab_in, pl.BlockSpec((None, piece, d), lambda i: (layer, jnp.minimum(i, nvalid - 1), 0))]
    out_specs = [slab_out, slab_out, pl.BlockSpec((piece, d), lambda i: (jnp.minimum(i, nvalid), 0))]
    out_shapes = [jax.ShapeDtypeStruct((d, fp), BF16), jax.ShapeDtypeStruct((d, fp), BF16),
                  jax.ShapeDtypeStruct((fp, d), BF16)]
    nbytes = 2 * (2 * rows * f * 4 + piece * d * 4) + 2 * (2 * rows * fp * 2 + piece * d * 2)
    return in_specs, out_specs, out_shapes, nbytes, nvalid


def _mix_out_kernel(h_ref, a_ref, gn_ref, ag_ref, wout_ref, w1_ref, w3_ref, w2_ref,
                    o_ref, c1_ref, c3_ref, c2_ref, *, nvalid):
    an = _rms(a_ref[...], ag_ref[...]).astype(BF16)
    mixed = jnp.concatenate([an, gn_ref[...]], axis=1)
    o_ref[...] = h_ref[...] + jnp.dot(mixed, wout_ref[...], preferred_element_type=F32)
    _ride_copy(w1_ref, w3_ref, w2_ref, c1_ref, c3_ref, c2_ref, nvalid)


def _mix_out(h, a, gn, ag, wout, w1, w3, w2, layer, *, tm, fp):
    t, d = h.shape
    aw, gw = a.shape[1], gn.shape[1]
    ride_in, ride_out, ride_shapes, ride_bytes, nvalid = _ride_specs(w1, layer, t // tm, fp)
    est = (aw + gw) * d * 2 + 2 * tm * (2 * d * 4 + aw * 4 + gw * 2) + 3 * tm * d * 4 + ride_bytes
    return pl.pallas_call(
        functools.partial(_mix_out_kernel, nvalid=nvalid),
        out_shape=(jax.ShapeDtypeStruct((t, d), F32), *ride_shapes),
        grid=(t // tm,),
        in_specs=[
            pl.BlockSpec((tm, d), lambda i: (i, 0)),
            pl.BlockSpec((tm, aw), lambda i: (i, 0)),
            pl.BlockSpec((tm, gw), lambda i: (i, 0)),
            pl.BlockSpec((1, aw), lambda i: (0, 0), pipeline_mode=pl.Buffered(1)),
            pl.BlockSpec((None, aw + gw, d), lambda i: (layer, 0, 0), pipeline_mode=pl.Buffered(1)),
            *ride_in,
        ],
        out_specs=(pl.BlockSpec((tm, d), lambda i: (i, 0)), *ride_out),
        compiler_params=pltpu.CompilerParams(
            dimension_semantics=("arbitrary",), vmem_limit_bytes=_vmem_limit(est)),
        name="mix_out",
    )(h, a, gn, ag, wout, w1, w3, w2)


def _ple_kernel(h_ref, p_ref, gg_ref, wg_ref, wp_ref, pg_ref, *refs, nvalid):
    o_ref = refs[0] if nvalid is None else refs[3]
    x = h_ref[...]
    e = _rms(jnp.dot(p_ref[...].astype(BF16), wp_ref[...], preferred_element_type=F32), pg_ref[...])
    gate = jax.nn.sigmoid(
        jnp.dot(_rms(x, gg_ref[...]).astype(BF16), wg_ref[...], preferred_element_type=F32))
    o_ref[...] = x + gate * e
    if nvalid is not None:
        _ride_copy(*refs[:3], *refs[4:], nvalid)


def _ple(h, p, layer, gg, wg, wp, pg, *, tm, ride=None):
    t, d = h.shape
    pd = p.shape[-1]
    est = (d + pd) * d * 2 + 2 * tm * (2 * d * 4 + pd * 4) + 4 * tm * d * 4
    ride_in, ride_out, ride_shapes, ride_args, nvalid = [], [], [], (), None
    if ride is not None:
        ride_in, ride_out, ride_shapes, ride_bytes, nvalid = _ride_specs(ride[0], ride[3], t // tm, ride[4])
        ride_args = ride[:3]
        est += ride_bytes
    out = pl.pallas_call(
        functools.partial(_ple_kernel, nvalid=nvalid),
        out_shape=(jax.ShapeDtypeStruct((t, d), F32), *ride_shapes),
        grid=(t // tm,),
        in_specs=[
            pl.BlockSpec((tm, d), lambda i: (i, 0)),
            pl.BlockSpec((None, tm, pd), lambda i: (layer, i, 0)),
            pl.BlockSpec((1, d), lambda i: (0, 0), pipeline_mode=pl.Buffered(1)),
            pl.BlockSpec((None, d, d), lambda i: (layer, 0, 0), pipeline_mode=pl.Buffered(1)),
            pl.BlockSpec((None, pd, d), lambda i: (layer, 0, 0), pipeline_mode=pl.Buffered(1)),
            pl.BlockSpec((1, d), lambda i: (0, 0), pipeline_mode=pl.Buffered(1)),
            *ride_in,
        ],
        out_specs=(pl.BlockSpec((tm, d), lambda i: (i, 0)), *ride_out),
        compiler_params=pltpu.CompilerParams(
            dimension_semantics=("arbitrary",), vmem_limit_bytes=_vmem_limit(est)),
        name="ple",
    )(h, p, gg, wg, wp, pg, *ride_args)
    return out[0], list(out[1:])


def _cast_pad_kernel(w_ref, o_ref, *, axis, size):
    x = w_ref[...].astype(BF16)
    if axis == 0:
        o_ref[:size, :] = x
        o_ref[size:, :] = jnp.zeros((o_ref.shape[0] - size, o_ref.shape[1]), BF16)
    else:
        o_ref[:, :size] = x
        o_ref[:, size:] = jnp.zeros((o_ref.shape[0], o_ref.shape[1] - size), BF16)


def _cast_pad(w, *, axis, padded, blk, layer):
    _, r, c = w.shape
    nl = 1
    if axis == 0:
        in_blk, out_blk, out_shape, grid = (None, r, blk), (None, padded, blk), (nl, padded, c), (nl, c // blk)
        idx = lambda l, j: (l, 0, j)
    else:
        in_blk, out_blk, out_shape, grid = (None, blk, c), (None, blk, padded), (nl, r, padded), (nl, r // blk)
        idx = lambda l, j: (l, j, 0)
    in_idx = lambda l, j: (layer,) + idx(l, j)[1:]
    in_bytes, out_bytes = (r, c)[axis] * blk * 4, padded * blk * 2
    est = 3 * in_bytes + 2 * out_bytes
    return pl.pallas_call(
        functools.partial(_cast_pad_kernel, axis=axis, size=(r, c)[axis]),
        out_shape=jax.ShapeDtypeStruct(out_shape, BF16),
        grid=grid,
        in_specs=[pl.BlockSpec(in_blk, in_idx)],
        out_specs=pl.BlockSpec(out_blk, idx),
        compiler_params=pltpu.CompilerParams(
            dimension_semantics=("parallel", "parallel"), vmem_limit_bytes=_vmem_limit(est)),
        name="cast_pad",
    )(w)


def _ffn_weights(w1, w3, w2, *, tf, layer):
    f = w1.shape[2]
    fp = -(-f // tf) * tf
    return (_cast_pad(w1, axis=1, padded=fp, blk=256, layer=layer),
            _cast_pad(w3, axis=1, padded=fp, blk=256, layer=layer),
            _cast_pad(w2, axis=0, padded=fp, blk=256, layer=layer))


def _dup_rope(x, axis_len_nope):
    return jnp.concatenate([x, x[..., axis_len_nope:]], axis=-1)


def kernel(x, p, positions, ffn_a_norm, ffn_a_w1, ffn_a_w3, ffn_a_w2, mix_norm, w_in, q_a_norm, w_uq, kv_a_norm, w_ukv, q_norm, k_norm, gm_v_norm, gm_ws, gm_bs, attn_out_norm, gm_out_norm, w_out, ffn_b_norm, ffn_b_w1, ffn_b_w3, ffn_b_w2, ple_gate_norm, w_ple_gate, w_ple, ple_norm):
    b, s, d = x.shape
    depth = p.shape[0]
    t = b * s
    tm_ffn, tf = 1024, 512
    ffn_a = _ffn_weights(ffn_a_w1, ffn_a_w3, ffn_a_w2, tf=tf, layer=0)
    fp = ffn_a[0].shape[2]
    wout_b, wgate_b, wple_b = w_out.astype(BF16), w_ple_gate.astype(BF16), w_ple.astype(BF16)
    tm_out, tm_ple, tq, heads_per_step = 512, 512, 512, 4

    cos, sin = _rope_tables(positions.astype(F32).reshape(t, 1), tm=1024)
    p2 = p.reshape(depth, t, p.shape[-1])
    h = x.reshape(t, d)
    o_kr = Q_LORA_RANK + KV_LORA_RANK
    for i in range(depth):
        h = _ffn(h, ffn_a_norm[i][None, :], *ffn_a, 0, tm=tm_ffn, tf=tf)

        win = w_in[i].astype(BF16)
        win = jnp.concatenate(
            [win[:, :o_kr + QK_ROPE_DIM], win[:, o_kr:o_kr + QK_ROPE_DIM], win[:, o_kr + QK_ROPE_DIM:]], axis=1)
        wuq = _dup_rope(w_uq[i].astype(BF16).reshape(Q_LORA_RANK, MLA_HEADS, QK_HEAD_DIM), QK_NOPE_DIM)
        wuq = wuq.reshape(Q_LORA_RANK, MLA_HEADS * HEAD_PAD)
        wukv = w_ukv[i].astype(BF16).reshape(KV_LORA_RANK, MLA_HEADS, QK_NOPE_DIM + V_HEAD_DIM)
        wuk = wukv[:, :, :QK_NOPE_DIM].reshape(KV_LORA_RANK, MLA_HEADS * QK_NOPE_DIM)
        wvt = wukv[:, :, QK_NOPE_DIM:].reshape(KV_LORA_RANK, MLA_HEADS * V_HEAD_DIM).T
        q, k, vt, gn = _mix_in(
            h, mix_norm[i][None, :], win, q_a_norm[i][None, :], wuq, kv_a_norm[i][None, :],
            wuk, wvt, _dup_rope(q_norm[i], QK_NOPE_DIM)[None, :],
            _dup_rope(k_norm[i], QK_NOPE_DIM)[None, :], gm_v_norm[i][None, :], gm_ws[i].astype(BF16),
            gm_bs[i].T, gm_out_norm[i][None, :], cos, sin, tm=tq)
        a = _attention(q.reshape(b, s, -1), k.reshape(b, s, -1), vt, tq=tq, hg=heads_per_step)
        h, *ffn_b = _mix_out(h, a.reshape(t, -1), gn, attn_out_norm[i][None, :], wout_b,
                             ffn_b_w1, ffn_b_w3, ffn_b_w2, i, tm=tm_out, fp=fp)

        h = _ffn(h, ffn_b_norm[i][None, :], *(w[None] for w in ffn_b), 0, tm=tm_ffn, tf=tf)

        ride = (ffn_a_w1, ffn_a_w3, ffn_a_w2, i + 1, fp) if i + 1 < depth else None
        h, nxt = _ple(h, p2, i, ple_gate_norm[i][None, :], wgate_b, wple_b, ple_norm[i][None, :],
                      tm=tm_ple, ride=ride)
        ffn_a = [w[None] for w in nxt]
    return h.reshape(b, s, d)
```

```python
import functools
import math

import jax
import jax.numpy as jnp
from jax import lax
from jax.experimental import pallas as pl
from jax.experimental.pallas import tpu as pltpu

F32 = jnp.float32
BF16 = jnp.bfloat16

MLA_HEADS = 8
QK_NOPE_DIM = 128
QK_ROPE_DIM = 64
QK_HEAD_DIM = QK_NOPE_DIM + QK_ROPE_DIM
V_HEAD_DIM = 128
Q_LORA_RANK = 512
KV_LORA_RANK = 256
GM_GROUPS = 8
GM_GROUP_DIM = 128
CHUNK = 128
ROPE_BASE = 10000.0
EPS = 1e-6

LANES = 128
V7X_VMEM_LIMIT_CAP = 60000 * 1024

HEAD_PAD = 2 * LANES
NEG_BIG = -1e30
SUM_ROWS = 16
LOG2E = math.log2(math.e)


def _vmem_limit(nbytes):
    assert nbytes <= V7X_VMEM_LIMIT_CAP, nbytes
    return int(nbytes)


def _rms(x, g):
    ms = jnp.mean(x * x, axis=-1, keepdims=True)
    return x * lax.rsqrt(ms + EPS) * g


def _rope_kernel(pos_ref, invf_ref, cmask_ref, smask_ref, cos_ref, sin_ref):
    ang = pos_ref[...] * invf_ref[...]
    cos_ref[...] = jnp.cos(ang) * cmask_ref[...]
    sin_ref[...] = jnp.sin(ang) * smask_ref[...]


def _rope_tables(pos_f, *, tm):
    t = pos_f.shape[0]
    half = QK_ROPE_DIM // 2
    inv_freq = ROPE_BASE ** (-jnp.arange(0, QK_ROPE_DIM, 2, dtype=F32) / QK_ROPE_DIM)
    invf = jnp.tile(inv_freq, LANES // half)[None, :]
    ones, zeros = jnp.ones((half,), F32), jnp.zeros((half,), F32)
    cmask = jnp.concatenate([ones, ones, zeros, zeros])[None, :]
    smask = jnp.concatenate([-ones, ones, zeros, zeros])[None, :]
    row = pl.BlockSpec((1, LANES), lambda i: (0, 0))
    tab = pl.BlockSpec((tm, LANES), lambda i: (i, 0))
    return pl.pallas_call(
        _rope_kernel,
        out_shape=(jax.ShapeDtypeStruct((t, LANES), F32), jax.ShapeDtypeStruct((t, LANES), F32)),
        grid=(t // tm,),
        in_specs=[pl.BlockSpec((tm, 1), lambda i: (i, 0)), row, row, row],
        out_specs=(tab, tab),
        compiler_params=pltpu.CompilerParams(dimension_semantics=("parallel",)),
        name="rope_tables",
    )(pos_f, invf, cmask, smask)


def _ffn_kernel(h_ref, g_ref, w1_ref, w3_ref, w2_ref, o_ref, n_ref):
    k = pl.program_id(1)

    def half_swiglu(n):
        a = jnp.dot(n, w1_ref[...], preferred_element_type=F32)
        b = jnp.dot(n, w3_ref[...], preferred_element_type=F32)
        mid = (0.5 * a) * jax.nn.sigmoid(a) * b
        return jnp.dot(mid.astype(BF16), w2_ref[...], preferred_element_type=F32)

    @pl.when(k == 0)
    def _():
        x = h_ref[...]
        n = _rms(x, g_ref[...]).astype(BF16)
        n_ref[...] = n
        o_ref[...] = x + half_swiglu(n)

    @pl.when(k > 0)
    def _():
        o_ref[...] += half_swiglu(n_ref[...])


def _ffn(h, g, w1, w3, w2, layer, *, tm, tf):
    t, d = h.shape
    f = w2.shape[1]
    est = 2 * tm * d * 4 + 2 * tm * d * 4 + tm * d * 2 + 2 * 3 * d * tf * 2 + 4 * tm * tf * 4
    return pl.pallas_call(
        _ffn_kernel,
        out_shape=jax.ShapeDtypeStruct((t, d), F32),
        grid=(t // tm, f // tf),
        in_specs=[
            pl.BlockSpec((tm, d), lambda i, k: (i, 0)),
            pl.BlockSpec((1, d), lambda i, k: (0, 0)),
            pl.BlockSpec((None, d, tf), lambda i, k: (layer, 0, k)),
            pl.BlockSpec((None, d, tf), lambda i, k: (layer, 0, k)),
            pl.BlockSpec((None, tf, d), lambda i, k: (layer, k, 0)),
        ],
        out_specs=pl.BlockSpec((tm, d), lambda i, k: (i, 0)),
        scratch_shapes=[pltpu.VMEM((tm, d), BF16)],
        compiler_params=pltpu.CompilerParams(
            dimension_semantics=("parallel", "arbitrary"), vmem_limit_bytes=_vmem_limit(est)),
        name="ffn",
    )(h, g, w1, w3, w2)


def _mix_in_kernel(h_ref, mixg_ref, win_ref, qag_ref, wuq_ref, kvag_ref, wuk_ref, wvt_ref, qg_ref, kg_ref,
                   gvg_ref, ws_ref, bs_ref, gog_ref, cos_ref, sin_ref,
                   q_ref, k_ref, vt_ref, gn_ref, *, tm):
    n = _rms(h_ref[...], mixg_ref[...]).astype(BF16)
    o_kv = Q_LORA_RANK
    o_kr = o_kv + KV_LORA_RANK
    o_u = o_kr + LANES
    o_v = o_u + GM_GROUPS * GM_GROUP_DIM
    zc = jnp.dot(n, win_ref[:, :o_u], preferred_element_type=F32)
    c_q, c_kv, kr = zc[:, :o_kv], zc[:, o_kv:o_kr], zc[:, o_kr:o_u]
    qf = jnp.dot(_rms(c_q, qag_ref[...]).astype(BF16), wuq_ref[...], preferred_element_type=F32)
    kvn = _rms(c_kv, kvag_ref[...]).astype(BF16)
    kf = jnp.dot(kvn, wuk_ref[...], preferred_element_type=F32)
    v = jnp.dot(n, win_ref[:, o_v:], preferred_element_type=F32)
    u = jnp.dot(n, win_ref[:, o_u:o_v], preferred_element_type=F32)
    cos, sin = cos_ref[...], sin_ref[...]
    inv_hd = 1.0 / QK_HEAD_DIM

    def rope(x):
        return x * cos + pltpu.roll(x, QK_ROPE_DIM // 2, 1) * sin

    qg = qg_ref[...]
    q_scale = QK_HEAD_DIM ** -0.5 * LOG2E
    for hd in range(MLA_HEADS):
        nope = qf[:, hd * HEAD_PAD: hd * HEAD_PAD + LANES]
        rt = qf[:, hd * HEAD_PAD + LANES: (hd + 1) * HEAD_PAD]
        ss = jnp.sum(nope * nope + 0.5 * (rt * rt), axis=-1, keepdims=True)
        r = lax.rsqrt(ss * inv_hd + EPS) * q_scale
        q_ref[:, hd * HEAD_PAD: hd * HEAD_PAD + LANES] = (nope * r * qg[:, :LANES]).astype(BF16)
        q_ref[:, hd * HEAD_PAD + LANES: (hd + 1) * HEAD_PAD] = rope(rt * r * qg[:, LANES:]).astype(BF16)

    kg = kg_ref[...]
    kr_rot = rope(kr * kg[:, LANES:])
    ss_r = 0.5 * jnp.sum(kr * kr, axis=-1, keepdims=True)
    for hd in range(MLA_HEADS):
        kn = kf[:, hd * QK_NOPE_DIM: (hd + 1) * QK_NOPE_DIM]
        ss = jnp.sum(kn * kn, axis=-1, keepdims=True) + ss_r
        r = lax.rsqrt(ss * inv_hd + EPS)
        k_ref[:, hd * HEAD_PAD: hd * HEAD_PAD + LANES] = (kn * r * kg[:, :LANES]).astype(BF16)
        k_ref[:, hd * HEAD_PAD + LANES: (hd + 1) * HEAD_PAD] = (kr_rot * r).astype(BF16)

    ug = jax.nn.gelu(u)
    vn = _rms(jax.nn.gelu(v), gvg_ref[...]).astype(BF16)
    nchunk = tm // CHUNK
    trow = lax.broadcasted_iota(jnp.int32, (CHUNK, CHUNK), 0)
    scol = lax.broadcasted_iota(jnp.int32, (CHUNK, CHUNK), 1)
    cols = []
    for g in range(GM_GROUPS):
        wc = jnp.where(scol <= trow, ws_ref[g], jnp.zeros((), BF16))
        lo, hi = g * GM_GROUP_DIM, (g + 1) * GM_GROUP_DIM
        rhs = jnp.concatenate([vn[c * CHUNK:(c + 1) * CHUNK, lo:hi] for c in range(nchunk)], axis=1)
        gate = jnp.dot(wc, rhs, preferred_element_type=F32) + bs_ref[:, g:g + 1]
        cols.append(jnp.concatenate(
            [gate[:, c * GM_GROUP_DIM:(c + 1) * GM_GROUP_DIM] for c in range(nchunk)], axis=0))
    vt_ref[0] = lax.dot_general(wvt_ref[...], kvn, (((1,), (1,)), ((), ())),
                                preferred_element_type=F32).astype(BF16)
    g_out = ug * jnp.concatenate(cols, axis=1)
    gn_ref[...] = _rms(g_out, gog_ref[...]).astype(BF16)


def _mix_in(h, mixg, win, qag, wuq, kvag, wuk, wvt, qg, kg, gvg, ws, bs, gog, cos, sin, *, tm):
    t, d = h.shape
    zc = win.shape[1]
    qw, kw, vw, gw = MLA_HEADS * HEAD_PAD, MLA_HEADS * QK_NOPE_DIM, MLA_HEADS * V_HEAD_DIM, GM_GROUPS * GM_GROUP_DIM

    def const(shape):
        return pl.BlockSpec(shape, lambda i: (0,) * len(shape), pipeline_mode=pl.Buffered(1))

    def rows(width):
        return pl.BlockSpec((tm, width), lambda i: (i, 0))

    weights = (d * zc + Q_LORA_RANK * qw + KV_LORA_RANK * (kw + vw) + GM_GROUPS * CHUNK * CHUNK) * 2
    est = weights + 2 * tm * d * 4 + 2 * tm * (2 * qw + vw + gw) * 2 + 4 * tm * LANES * 4 + 3 * tm * zc * 4
    return pl.pallas_call(
        functools.partial(_mix_in_kernel, tm=tm),
        out_shape=(jax.ShapeDtypeStruct((t, qw), BF16), jax.ShapeDtypeStruct((t, qw), BF16),
                   jax.ShapeDtypeStruct((t // tm, vw, tm), BF16), jax.ShapeDtypeStruct((t, gw), BF16)),
        grid=(t // tm,),
        in_specs=[rows(d), const((1, d)), const((d, zc)), const((1, Q_LORA_RANK)), const((Q_LORA_RANK, qw)),
                  const((1, KV_LORA_RANK)), const((KV_LORA_RANK, kw)), const((vw, KV_LORA_RANK)),
                  const((1, HEAD_PAD)), const((1, HEAD_PAD)),
                  const((1, gw)), const((GM_GROUPS, CHUNK, CHUNK)), const((CHUNK, GM_GROUPS)), const((1, gw)),
                  rows(LANES), rows(LANES)],
        out_specs=(rows(qw), rows(qw), pl.BlockSpec((1, vw, tm), lambda i: (i, 0, 0)), rows(gw)),
        compiler_params=pltpu.CompilerParams(
            dimension_semantics=("parallel",), vmem_limit_bytes=_vmem_limit(est)),
        name="mix_in",
    )(h, mixg, win, qag, wuq, kvag, wuk, wvt, qg, kg, gvg, ws, bs, gog, cos, sin)


def _attn_kernel(q_ref, k_ref, vt_ref, o_ref, st_ref, *acc_refs, tq, hg):
    i = pl.program_id(2)
    for acc_ref in acc_refs:
        acc_ref[...] = jnp.zeros(acc_ref.shape, F32)

    ones = jnp.ones((SUM_ROWS, tq), BF16)

    def scores(j, slot):
        start = pl.multiple_of(j * tq, tq)
        for hd in range(hg):
            q = q_ref[0, :, hd * HEAD_PAD:(hd + 1) * HEAD_PAD]
            kj = k_ref[0, pl.ds(start, tq), hd * HEAD_PAD:(hd + 1) * HEAD_PAD]
            st_ref[slot, hd] = lax.dot_general(kj, q, (((1,), (1,)), ((), ())), preferred_element_type=F32)

    def softmax(ms, slot, diagonal):
        if diagonal:
            key = lax.broadcasted_iota(jnp.int32, (tq, tq), 0)
            qry = lax.broadcasted_iota(jnp.int32, (tq, tq), 1)
            keep = key <= qry
        new_ms, pts, alphas = [], [], []
        for hd in range(hg):
            st = jnp.where(keep, st_ref[slot, hd], NEG_BIG) if diagonal else st_ref[slot, hd]
            m_new = jnp.maximum(ms[hd], jnp.max(st, axis=0, keepdims=True))
            pts.append(jnp.exp2(st - m_new).astype(BF16))
            alphas.append(jnp.exp2(ms[hd] - m_new))
            new_ms.append(m_new)
        return tuple(new_ms), pts, alphas

    def accumulate(j, pts, alphas):
        for hd in range(hg):
            vt1 = jnp.concatenate([vt_ref[j, hd * V_HEAD_DIM:(hd + 1) * V_HEAD_DIM, :], ones], axis=0)
            acc_refs[hd][...] = alphas[hd] * acc_refs[hd][...] + jnp.dot(
                vt1, pts[hd], preferred_element_type=F32)

    def run(j0, n, ms, diagonal=False):
        scores(j0, 0)
        for b in range(n):
            if b + 1 < n:
                scores(j0 + b + 1, (b + 1) % 2)
            ms, pts, alphas = softmax(ms, b % 2, diagonal and b == n - 1)
            accumulate(j0 + b, pts, alphas)
        return ms

    init = tuple(jnp.full((1, tq), NEG_BIG, F32) for _ in range(hg))
    ms = lax.fori_loop(0, lax.shift_right_logical(i, 2), lambda jj, m: run(4 * jj, 4, m), init)
    done = i - i % 4
    ms = lax.cond(i % 4 >= 2, lambda m: run(done, 2, m), lambda m: m, ms)
    ms = lax.cond(i % 2 == 1, lambda m: run(i - 1, 1, m), lambda m: m, ms)
    run(i, 1, ms, diagonal=True)
    for hd in range(hg):
        acc = acc_refs[hd][...]
        o_ref[0, :, hd * V_HEAD_DIM:(hd + 1) * V_HEAD_DIM] = (
            acc[:V_HEAD_DIM] / acc[V_HEAD_DIM:V_HEAD_DIM + 1]).T


def _attention(q, k, vt, *, tq, hg):
    b, s, _ = q.shape
    nkv = s // tq
    est = 2 * hg * tq * HEAD_PAD * 2 + 2 * hg * s * (HEAD_PAD + V_HEAD_DIM) * 2 + 2 * hg * tq * V_HEAD_DIM * 4 \
        + hg * tq * (V_HEAD_DIM + SUM_ROWS) * 4 + 2 * hg * tq * tq * 4 + 2 * hg * tq * tq * 4
    return pl.pallas_call(
        functools.partial(_attn_kernel, tq=tq, hg=hg),
        out_shape=jax.ShapeDtypeStruct((b, s, MLA_HEADS * V_HEAD_DIM), F32),
        grid=(b, MLA_HEADS // hg, nkv),
        in_specs=[
            pl.BlockSpec((1, tq, hg * HEAD_PAD), lambda bi, hi, qi: (bi, qi, hi)),
            pl.BlockSpec((1, s, hg * HEAD_PAD), lambda bi, hi, qi: (bi, 0, hi)),
            pl.BlockSpec((nkv, hg * V_HEAD_DIM, tq), lambda bi, hi, qi: (bi, hi, 0)),
        ],
        out_specs=pl.BlockSpec((1, tq, hg * V_HEAD_DIM), lambda bi, hi, qi: (bi, qi, hi)),
        scratch_shapes=[pltpu.VMEM((2, hg, tq, tq), F32)]
        + [pltpu.VMEM((V_HEAD_DIM + SUM_ROWS, tq), F32) for _ in range(hg)],
        compiler_params=pltpu.CompilerParams(
            dimension_semantics=("parallel", "parallel", "arbitrary"), vmem_limit_bytes=_vmem_limit(est)),
        name="attn",
    )(q, k, vt)


def _mix_out_kernel(h_ref, a_ref, gn_ref, ag_ref, wout_ref, w1_ref, w3_ref, w2_ref,
                    o_ref, c1_ref, c3_ref, c2_ref, *, nvalid):
    an = _rms(a_ref[...], ag_ref[...]).astype(BF16)
    mixed = jnp.concatenate([an, gn_ref[...]], axis=1)
    o_ref[...] = h_ref[...] + jnp.dot(mixed, wout_ref[...], preferred_element_type=F32)
    f = w1_ref.shape[1]
    for w_ref, c_ref in ((w1_ref, c1_ref), (w3_ref, c3_ref)):
        c_ref[:, :f] = w_ref[...].astype(BF16)
        c_ref[:, f:] = jnp.zeros((c_ref.shape[0], c_ref.shape[1] - f), BF16)
    piece = w2_ref[...].astype(BF16)
    c2_ref[...] = jnp.where(pl.program_id(0) < nvalid, piece, jnp.zeros_like(piece))


def _mix_out(h, a, gn, ag, wout, w1, w3, w2, layer, *, tm, fp):
    t, d = h.shape
    aw, gw = a.shape[1], gn.shape[1]
    f = w1.shape[2]
    nsteps = t // tm
    rows, piece = d // nsteps, fp - f
    nvalid = f // piece
    assert d % nsteps == 0 and rows % SUM_ROWS == 0 and f % piece == 0 and fp // piece <= nsteps
    ride = 2 * (2 * rows * f * 4 + piece * d * 4) + 2 * (2 * rows * fp * 2 + piece * d * 2)
    est = (aw + gw) * d * 2 + 2 * tm * (2 * d * 4 + aw * 4 + gw * 2) + 3 * tm * d * 4 + ride
    slab_in = pl.BlockSpec((None, rows, f), lambda i: (layer, i, 0))
    slab_out = pl.BlockSpec((rows, fp), lambda i: (i, 0))
    return pl.pallas_call(
        functools.partial(_mix_out_kernel, nvalid=nvalid),
        out_shape=(jax.ShapeDtypeStruct((t, d), F32), jax.ShapeDtypeStruct((d, fp), BF16),
                   jax.ShapeDtypeStruct((d, fp), BF16), jax.ShapeDtypeStruct((fp, d), BF16)),
        grid=(nsteps,),
        in_specs=[
            pl.BlockSpec((tm, d), lambda i: (i, 0)),
            pl.BlockSpec((tm, aw), lambda i: (i, 0)),
            pl.BlockSpec((tm, gw), lambda i: (i, 0)),
            pl.BlockSpec((1, aw), lambda i: (0, 0), pipeline_mode=pl.Buffered(1)),
            pl.BlockSpec((None, aw + gw, d), lambda i: (layer, 0, 0), pipeline_mode=pl.Buffered(1)),
            slab_in, slab_in,
            pl.BlockSpec((None, piece, d), lambda i: (layer, jnp.minimum(i, nvalid - 1), 0)),
        ],
        out_specs=(pl.BlockSpec((tm, d), lambda i: (i, 0)), slab_out, slab_out,
                   pl.BlockSpec((piece, d), lambda i: (jnp.minimum(i, nvalid), 0))),
        compiler_params=pltpu.CompilerParams(
            dimension_semantics=("arbitrary",), vmem_limit_bytes=_vmem_limit(est)),
        name="mix_out",
    )(h, a, gn, ag, wout, w1, w3, w2)


def _ple_kernel(h_ref, p_ref, gg_ref, wg_ref, wp_ref, pg_ref, o_ref):
    x = h_ref[...]
    e = _rms(jnp.dot(p_ref[...].astype(BF16), wp_ref[...], preferred_element_type=F32), pg_ref[...])
    gate = jax.nn.sigmoid(
        jnp.dot(_rms(x, gg_ref[...]).astype(BF16), wg_ref[...], preferred_element_type=F32))
    o_ref[...] = x + gate * e


def _ple(h, p, layer, gg, wg, wp, pg, *, tm):
    t, d = h.shape
    pd = p.shape[-1]
    est = (d + pd) * d * 2 + 2 * tm * (2 * d * 4 + pd * 4) + 4 * tm * d * 4
    return pl.pallas_call(
        _ple_kernel,
        out_shape=jax.ShapeDtypeStruct((t, d), F32),
        grid=(t // tm,),
        in_specs=[
            pl.BlockSpec((tm, d), lambda i: (i, 0)),
            pl.BlockSpec((None, tm, pd), lambda i: (layer, i, 0)),
            pl.BlockSpec((1, d), lambda i: (0, 0), pipeline_mode=pl.Buffered(1)),
            pl.BlockSpec((None, d, d), lambda i: (layer, 0, 0), pipeline_mode=pl.Buffered(1)),
            pl.BlockSpec((None, pd, d), lambda i: (layer, 0, 0), pipeline_mode=pl.Buffered(1)),
            pl.BlockSpec((1, d), lambda i: (0, 0), pipeline_mode=pl.Buffered(1)),
        ],
        out_specs=pl.BlockSpec((tm, d), lambda i: (i, 0)),
        compiler_params=pltpu.CompilerParams(
            dimension_semantics=("parallel",), vmem_limit_bytes=_vmem_limit(est)),
        name="ple",
    )(h, p, gg, wg, wp, pg)


def _cast_pad_kernel(w_ref, o_ref, *, axis, size):
    x = w_ref[...].astype(BF16)
    if axis == 0:
        o_ref[:size, :] = x
        o_ref[size:, :] = jnp.zeros((o_ref.shape[0] - size, o_ref.shape[1]), BF16)
    else:
        o_ref[:, :size] = x
        o_ref[:, size:] = jnp.zeros((o_ref.shape[0], o_ref.shape[1] - size), BF16)


def _cast_pad(w, *, axis, padded, blk):
    nl, r, c = w.shape
    if axis == 0:
        in_blk, out_blk, out_shape, grid = (None, r, blk), (None, padded, blk), (nl, padded, c), (nl, c // blk)
        idx = lambda l, j: (l, 0, j)
    else:
        in_blk, out_blk, out_shape, grid = (None, blk, c), (None, blk, padded), (nl, r, padded), (nl, r // blk)
        idx = lambda l, j: (l, j, 0)
    in_bytes, out_bytes = (r, c)[axis] * blk * 4, padded * blk * 2
    est = 3 * in_bytes + 2 * out_bytes
    return pl.pallas_call(
        functools.partial(_cast_pad_kernel, axis=axis, size=(r, c)[axis]),
        out_shape=jax.ShapeDtypeStruct(out_shape, BF16),
        grid=grid,
        in_specs=[pl.BlockSpec(in_blk, idx)],
        out_specs=pl.BlockSpec(out_blk, idx),
        compiler_params=pltpu.CompilerParams(
            dimension_semantics=("parallel", "parallel"), vmem_limit_bytes=_vmem_limit(est)),
        name="cast_pad",
    )(w)


def _ffn_weights(w1, w3, w2, *, tf):
    f = w1.shape[2]
    fp = -(-f // tf) * tf
    return (_cast_pad(w1, axis=1, padded=fp, blk=256), _cast_pad(w3, axis=1, padded=fp, blk=256),
            _cast_pad(w2, axis=0, padded=fp, blk=256))


def _dup_rope(x, axis_len_nope):
    return jnp.concatenate([x, x[..., axis_len_nope:]], axis=-1)


def kernel(x, p, positions, ffn_a_norm, ffn_a_w1, ffn_a_w3, ffn_a_w2, mix_norm, w_in, q_a_norm, w_uq, kv_a_norm, w_ukv, q_norm, k_norm, gm_v_norm, gm_ws, gm_bs, attn_out_norm, gm_out_norm, w_out, ffn_b_norm, ffn_b_w1, ffn_b_w3, ffn_b_w2, ple_gate_norm, w_ple_gate, w_ple, ple_norm):
    b, s, d = x.shape
    depth = p.shape[0]
    t = b * s
    tm_ffn, tf = 1024, 512
    ffn_a = _ffn_weights(ffn_a_w1, ffn_a_w3, ffn_a_w2, tf=tf)
    wout_b, wgate_b, wple_b = w_out.astype(BF16), w_ple_gate.astype(BF16), w_ple.astype(BF16)
    tm_out, tm_ple, tq, heads_per_step = 512, 512, 512, 4

    cos, sin = _rope_tables(positions.astype(F32).reshape(t, 1), tm=1024)
    p2 = p.reshape(depth, t, p.shape[-1])
    h = x.reshape(t, d)
    o_kr = Q_LORA_RANK + KV_LORA_RANK
    for i in range(depth):
        h = _ffn(h, ffn_a_norm[i][None, :], *ffn_a, i, tm=tm_ffn, tf=tf)

        win = w_in[i].astype(BF16)
        win = jnp.concatenate(
            [win[:, :o_kr + QK_ROPE_DIM], win[:, o_kr:o_kr + QK_ROPE_DIM], win[:, o_kr + QK_ROPE_DIM:]], axis=1)
        wuq = _dup_rope(w_uq[i].astype(BF16).reshape(Q_LORA_RANK, MLA_HEADS, QK_HEAD_DIM), QK_NOPE_DIM)
        wuq = wuq.reshape(Q_LORA_RANK, MLA_HEADS * HEAD_PAD)
        wukv = w_ukv[i].astype(BF16).reshape(KV_LORA_RANK, MLA_HEADS, QK_NOPE_DIM + V_HEAD_DIM)
        wuk = wukv[:, :, :QK_NOPE_DIM].reshape(KV_LORA_RANK, MLA_HEADS * QK_NOPE_DIM)
        wvt = wukv[:, :, QK_NOPE_DIM:].reshape(KV_LORA_RANK, MLA_HEADS * V_HEAD_DIM).T
        q, k, vt, gn = _mix_in(
            h, mix_norm[i][None, :], win, q_a_norm[i][None, :], wuq, kv_a_norm[i][None, :],
            wuk, wvt, _dup_rope(q_norm[i], QK_NOPE_DIM)[None, :],
            _dup_rope(k_norm[i], QK_NOPE_DIM)[None, :], gm_v_norm[i][None, :], gm_ws[i].astype(BF16),
            gm_bs[i].T, gm_out_norm[i][None, :], cos, sin, tm=tq)
        a = _attention(q.reshape(b, s, -1), k.reshape(b, s, -1), vt, tq=tq, hg=heads_per_step)
        h, *ffn_b = _mix_out(h, a.reshape(t, -1), gn, attn_out_norm[i][None, :], wout_b,
                             ffn_b_w1, ffn_b_w3, ffn_b_w2, i, tm=tm_out, fp=ffn_a[0].shape[2])

        h = _ffn(h, ffn_b_norm[i][None, :], *(w[None] for w in ffn_b), 0, tm=tm_ffn, tf=tf)

        h = _ple(h, p2, i, ple_gate_norm[i][None, :], wgate_b, wple_b, ple_norm[i][None, :], tm=tm_ple)
    return h.reshape(b, s, d)
```
